```python
import math
import jax, jax.numpy as jnp
from jax import lax
import numpy as np

D_MODEL = 1024
BATCH = 8
SEQ = 4096
DEPTH = 1
DEC_BATCH = 128
DEC_SEQ = 4
PAST_LEN = 16384
PAGE_SIZE = 128

N_META = 16
EPS = 1e-6
HEAD_DIM = 64
N_HEADS_A = D_MODEL // 2 // HEAD_DIM
N_KV_HEADS = N_HEADS_A // 4
GROUP = N_HEADS_A // N_KV_HEADS
WINDOW = 128
BLOCK = 128
ROPE_THETA = 10000.0
SWA_CACHE = min(WINDOW, PAST_LEN)
DK = 128
DV = 128
N_HEADS_B = D_MODEL // 2 // DV
CONV_W = 4
CHUNK = 64
D_FF = -(-8 * D_MODEL // (3 * 256)) * 256

Q_A = N_HEADS_A * HEAD_DIM
KV_A = N_KV_HEADS * HEAD_DIM
QK_B = N_HEADS_B * DK
V_B = N_HEADS_B * DV
CONV_DIM = 2 * QK_B + V_B
IN_SIZES = (Q_A, KV_A, KV_A, CONV_DIM, V_B, N_HEADS_B, N_HEADS_B, 2 * D_MODEL)
IN_DIM = sum(IN_SIZES)

kernel_name = 'hybrid_swa_sink_gated_deltanet_step'


def _rms_norm(x, w):
    xf = x.astype(jnp.float32)
    y = xf * lax.rsqrt(jnp.mean(xf * xf, axis=-1, keepdims=True) + EPS)
    return (y * w.astype(jnp.float32)).astype(x.dtype)


def _gated_rms_norm(o, z, w):
    of = o.astype(jnp.float32)
    y = of * lax.rsqrt(jnp.mean(of * of, axis=-1, keepdims=True) + EPS) * w.astype(jnp.float32)
    return (y * jax.nn.silu(z.astype(jnp.float32))).astype(z.dtype)


def _l2norm(x):
    return x * lax.rsqrt(jnp.sum(x * x, axis=-1, keepdims=True) + EPS)


def _rope(x, pos):
    half = HEAD_DIM // 2
    inv = ROPE_THETA ** (-jnp.arange(half, dtype=jnp.float32) / half)
    ang = pos.astype(jnp.float32)[:, None] * inv[None, :]
    cos = jnp.cos(ang)[None, :, None, :]
    sin = jnp.sin(ang)[None, :, None, :]
    xf = x.astype(jnp.float32)
    x1, x2 = xf[..., :half], xf[..., half:]
    return jnp.concatenate([x1 * cos - x2 * sin, x2 * cos + x1 * sin], axis=-1).astype(x.dtype)


def _sink_softmax(s, sinks, mask):
    sk = sinks.astype(jnp.float32).reshape(N_KV_HEADS, GROUP, 1, 1)
    s = jnp.where(mask, s, -jnp.inf)
    m = jnp.maximum(jnp.max(s, axis=-1, keepdims=True), sk)
    p = jnp.exp(s - m)
    return p / (jnp.sum(p, axis=-1, keepdims=True) + jnp.exp(sk - m))


def _swa_prompt(q, k, v, sinks):
    B, T = q.shape[:2]
    nb = -(-T // BLOCK)
    pad = ((0, 0), (0, nb * BLOCK - T), (0, 0), (0, 0))
    qb = jnp.pad(q, pad).reshape(B, nb, BLOCK, N_KV_HEADS, GROUP, HEAD_DIM)
    kb = jnp.pad(k, pad).reshape(B, nb, BLOCK, N_KV_HEADS, HEAD_DIM)
    vb = jnp.pad(v, pad).reshape(B, nb, BLOCK, N_KV_HEADS, HEAD_DIM)

    def band(x):
        prev = jnp.concatenate([jnp.zeros_like(x[:, :1]), x[:, :-1]], axis=1)
        return jnp.concatenate([prev, x], axis=2)

    kband, vband = band(kb), band(vb)
    qi = jnp.arange(BLOCK)[:, None]
    kj = jnp.arange(2 * BLOCK)[None, :]
    rel = qi + BLOCK - kj
    in_win = (rel >= 0) & (rel <= WINDOW)
    first = jnp.arange(nb)[:, None, None] == 0
    mask = in_win[None] & ~(first & (kj < BLOCK)[None])
    s = jnp.einsum('bnqkgd,bnskd->bnkgqs', qb, kband, preferred_element_type=jnp.float32) * (HEAD_DIM ** -0.5)
    p = _sink_softmax(s, sinks, mask[None, :, None, None])
    o = jnp.einsum('bnkgqs,bnskd->bnqkgd', p.astype(v.dtype), vband)
    return o.reshape(B, nb * BLOCK, Q_A)[:, :T]


def _swa_sample(q, k, v, k_past, v_past, sinks):
    B, L = q.shape[:2]
    kc = jnp.concatenate([k_past.astype(k.dtype), k], axis=1)
    vc = jnp.concatenate([v_past.astype(v.dtype), v], axis=1)
    qg = q.reshape(B, L, N_KV_HEADS, GROUP, HEAD_DIM)
    qi = jnp.arange(L)[:, None]
    kj = jnp.arange(SWA_CACHE + L)[None, :]
    rel = qi + SWA_CACHE - kj
    mask = (rel >= 0) & (rel <= WINDOW)
    s = jnp.einsum('bqkgd,bskd->bkgqs', qg, kc, preferred_element_type=jnp.float32) * (HEAD_DIM ** -0.5)
    p = _sink_softmax(s, sinks, mask)
    o = jnp.einsum('bkgqs,bskd->bqkgd', p.astype(v.dtype), vc).reshape(B, L, Q_A)
    return o, kc[:, -SWA_CACHE:], vc[:, -SWA_CACHE:]


def _short_conv(u, past, w):
    T = u.shape[1]
    up = jnp.concatenate([past.astype(u.dtype), u], axis=1)
    y = up[:, 0:T] * w[0]
    for i in range(1, CONV_W):
        y = y + up[:, i:i + T] * w[i]
    return jax.nn.silu(y), up[:, T:]


def _gated_delta_chunk(S, q, k, v, g, beta):
    L = q.shape[2]
    gc = jnp.cumsum(g, axis=-1)
    incl = jnp.tril(jnp.ones((L, L), dtype=bool))
    strict = jnp.tril(jnp.ones((L, L), dtype=bool), -1)
    diff = gc[..., :, None] - gc[..., None, :]
    decay = jnp.where(incl, jnp.exp(jnp.where(incl, diff, 0.0)), 0.0)
    kk = jnp.einsum('bhid,bhjd->bhij', k, k)
    a_mat = jnp.where(strict, beta[..., :, None] * decay * kk, 0.0) + jnp.eye(L, dtype=jnp.float32)
    eg = jnp.exp(gc)
    rhs = beta[..., None] * (v - eg[..., None] * jnp.einsum('bhid,bhde->bhie', k, S))
    u = lax.linalg.triangular_solve(a_mat, rhs, left_side=True, lower=True, unit_diagonal=True)
    qk = jnp.einsum('bhid,bhjd->bhij', q, k) * decay
    o = eg[..., None] * jnp.einsum('bhid,bhde->bhie', q, S) + jnp.einsum('bhij,bhje->bhie', qk, u)
    w_last = jnp.exp(gc[..., -1:] - gc)
    S_new = jnp.exp(gc[..., -1])[..., None, None] * S + jnp.einsum('bhjd,bhj,bhje->bhde', k, w_last, u)
    return S_new, o


def _delta_prompt(S0, q, k, v, g, beta):
    m = N_META
    S, o_meta = _gated_delta_chunk(S0, q[:, :, :m], k[:, :, :m], v[:, :, :m], g[:, :, :m], beta[:, :, :m])

    def to_chunks(x):
        x = x[:, :, m:]
        B, H, T = x.shape[:3]
        x = x.reshape((B, H, T // CHUNK, CHUNK) + x.shape[3:])
        return jnp.moveaxis(x, 2, 0)

    xs = (to_chunks(q), to_chunks(k), to_chunks(v), to_chunks(g), to_chunks(beta))
    S, o = lax.scan(lambda c, t: _gated_delta_chunk(c, *t), S, xs)
    o = jnp.moveaxis(o, 0, 2)
    o = o.reshape(o.shape[0], o.shape[1], -1, DV)
    return S, jnp.concatenate([o_meta, o], axis=2)


def _layer(h, pos, k_past, v_past, s0, conv_past, w, prompt):
    (norm_mix, w_in, conv_w, a_log, dt_bias, sinks, gdn_norm,
     w_oa, w_ob, w_out, norm_ffn, w_gate, w_up, w_down) = w
    B, T, _ = h.shape
    xn = _rms_norm(h, norm_mix)
    split_idx = np.cumsum(IN_SIZES)[:-1].tolist()
    qa, ka, va, qkv_b, z, a, b, gate_logits = jnp.split(xn @ w_in, split_idx, axis=-1)

    qa = _rope(qa.reshape(B, T, N_HEADS_A, HEAD_DIM), pos)
    ka = _rope(ka.reshape(B, T, N_KV_HEADS, HEAD_DIM), pos)
    va = va.reshape(B, T, N_KV_HEADS, HEAD_DIM)
    if prompt:
        oa = _swa_prompt(qa, ka, va, sinks)
        k_state, v_state = ka[:, -SWA_CACHE:], va[:, -SWA_CACHE:]
    else:
        oa, k_state, v_state = _swa_sample(qa, ka, va, k_past, v_past, sinks)

    cb, conv_state = _short_conv(qkv_b, conv_past, conv_w)
    qb, kb, vb = jnp.split(cb, [QK_B, 2 * QK_B], axis=-1)

    def heads(x, d):
        return jnp.swapaxes(x.reshape(B, T, N_HEADS_B, d), 1, 2).astype(jnp.float32)

    qb = _l2norm(heads(qb, DK)) * (DK ** -0.5)
    kb = _l2norm(heads(kb, DK))
    vb = heads(vb, DV)
    beta = jnp.swapaxes(jax.nn.sigmoid(b.astype(jnp.float32)), 1, 2)
    g = jnp.swapaxes(-jnp.exp(a_log.astype(jnp.float32))
                     * jax.nn.softplus(a.astype(jnp.float32) + dt_bias.astype(jnp.float32)), 1, 2)
    if prompt:
        s_new, ob = _delta_prompt(s0.astype(jnp.float32), qb, kb, vb, g, beta)
    else:
        s_new, ob = _gated_delta_chunk(s0.astype(jnp.float32), qb, kb, vb, g, beta)
    ob = jnp.swapaxes(ob, 1, 2).astype(h.dtype)
    ob = _gated_rms_norm(ob, z.reshape(B, T, N_HEADS_B, DV), gdn_norm).reshape(B, T, V_B)

    ga, gb = jnp.split(jax.nn.sigmoid(gate_logits), 2, axis=-1)
    h = h + (ga * (oa @ w_oa) + gb * (ob @ w_ob)) @ w_out

    hn = _rms_norm(h, norm_ffn)
    h = h + (jax.nn.silu(hn @ w_gate) * (hn @ w_up)) @ w_down
    return h, k_state, v_state, s_new.astype(h.dtype), conv_state


def setup_inputs(seed: int = 0) -> dict:
    key = jax.random.key(seed)
    ks = jax.random.split(key, 24)
    f32 = jnp.float32

    def nrm(k, shape, scale):
        return jax.random.normal(k, shape, f32) * scale

    dt = jnp.exp(jax.random.uniform(ks[11], (DEPTH, N_HEADS_B), f32, math.log(1e-3), math.log(1e-1)))
    return {
        'x_prompt': nrm(ks[0], (BATCH, SEQ, D_MODEL), 1.0),
        'x_sample': nrm(ks[1], (DEC_BATCH, DEC_SEQ, D_MODEL), 1.0),
        'cache_swa_k': nrm(ks[2], (DEPTH, DEC_BATCH, SWA_CACHE, N_KV_HEADS, HEAD_DIM), 1.0),
        'cache_swa_v': nrm(ks[3], (DEPTH, DEC_BATCH, SWA_CACHE, N_KV_HEADS, HEAD_DIM), 1.0),
        'state_delta': nrm(ks[4], (DEPTH, DEC_BATCH, N_HEADS_B, DK, DV), 0.1),
        'state_conv': nrm(ks[5], (DEPTH, DEC_BATCH, CONV_W - 1, CONV_DIM), 1.0),
        'meta_tokens': nrm(ks[6], (N_META, D_MODEL), 1.0),
        'norm_mix': 1.0 + nrm(ks[7], (DEPTH, D_MODEL), 0.02),
        'w_in': nrm(ks[8], (DEPTH, D_MODEL, IN_DIM), D_MODEL ** -0.5),
        'conv_w': nrm(ks[9], (DEPTH, CONV_W, CONV_DIM), CONV_W ** -0.5),
        'a_log': jnp.log(jax.random.uniform(ks[10], (DEPTH, N_HEADS_B), f32, 1.0, 16.0)),
        'dt_bias': dt + jnp.log(-jnp.expm1(-dt)),
        'sinks': nrm(ks[12], (DEPTH, N_HEADS_A), 0.5),
        'gdn_norm': 1.0 + nrm(ks[13], (DEPTH, DV), 0.02),
        'w_oa': nrm(ks[14], (DEPTH, Q_A, D_MODEL), Q_A ** -0.5),
        'w_ob': nrm(ks[15], (DEPTH, V_B, D_MODEL), V_B ** -0.5),
        'w_out': nrm(ks[16], (DEPTH, D_MODEL, D_MODEL), D_MODEL ** -0.5),
        'norm_ffn': 1.0 + nrm(ks[17], (DEPTH, D_MODEL), 0.02),
        'w_gate': nrm(ks[18], (DEPTH, D_MODEL, D_FF), D_MODEL ** -0.5),
        'w_up': nrm(ks[19], (DEPTH, D_MODEL, D_FF), D_MODEL ** -0.5),
        'w_down': nrm(ks[20], (DEPTH, D_FF, D_MODEL), D_FF ** -0.5),
        'norm_final': 1.0 + nrm(ks[21], (D_MODEL,), 0.02),
    }


def reference(x_prompt, x_sample, cache_swa_k, cache_swa_v, state_delta, state_conv,
              meta_tokens, norm_mix, w_in, conv_w, a_log, dt_bias, sinks, gdn_norm,
              w_oa, w_ob, w_out, norm_ffn, w_gate, w_up, w_down, norm_final):
    B = x_prompt.shape[0]
    meta = jnp.broadcast_to(meta_tokens.astype(x_prompt.dtype)[None], (B, N_META, D_MODEL))
    hp = jnp.concatenate([meta, x_prompt], axis=1)
    hs = x_sample
    pos_p = jnp.arange(hp.shape[1], dtype=jnp.int32)
    pos_s = PAST_LEN + jnp.arange(hs.shape[1], dtype=jnp.int32)
    kp_l, vp_l, sp_l, cp_l = [], [], [], []
    ks_l, vs_l, ss_l, cs_l = [], [], [], []
    for l in range(DEPTH):
        w = (norm_mix[l], w_in[l], conv_w[l], a_log[l], dt_bias[l], sinks[l], gdn_norm[l],
             w_oa[l], w_ob[l], w_out[l], norm_ffn[l], w_gate[l], w_up[l], w_down[l])
        s0_p = jnp.zeros((B, N_HEADS_B, DK, DV), jnp.float32)
        c0_p = jnp.zeros((B, CONV_W - 1, CONV_DIM), hp.dtype)
        hp, kp, vp, sp, cp = _layer(hp, pos_p, None, None, s0_p, c0_p, w, True)
        hs, kn, vn, sn, cn = _layer(hs, pos_s, cache_swa_k[l], cache_swa_v[l], state_delta[l], state_conv[l], w, False)
        kp_l.append(kp); vp_l.append(vp); sp_l.append(sp); cp_l.append(cp)
        ks_l.append(kn); vs_l.append(vn); ss_l.append(sn); cs_l.append(cn)
    y_prompt = _rms_norm(hp, norm_final)[:, N_META:]
    y_sample = _rms_norm(hs, norm_final)
    return (y_prompt, y_sample,
            jnp.stack(kp_l), jnp.stack(vp_l), jnp.stack(sp_l), jnp.stack(cp_l),
            jnp.stack(ks_l), jnp.stack(vs_l), jnp.stack(ss_l), jnp.stack(cs_l))
```

```python
import functools

import jax
import jax.numpy as jnp
from jax import lax
from jax.experimental import pallas as pl
from jax.experimental.pallas import tpu as pltpu

F32 = jnp.float32
BF16 = jnp.bfloat16

D_MODEL = 1024
N_META = 16
EPS = 1e-6
HEAD_DIM = 64
N_HEADS_A = 8
N_KV_HEADS = 2
BLK = 128
ROPE_THETA = 10000.0
PAST_LEN = 16384
DK = 128
N_HEADS_B = 4
CONV_W = 4
D_FF = 2816
Q_A = N_HEADS_A * HEAD_DIM
KV_A = N_KV_HEADS * HEAD_DIM
QK_B = N_HEADS_B * DK
V_B = N_HEADS_B * DK
CONV_DIM = 2 * QK_B + V_B
LANES = 128
PAD_ROWS = BLK - N_META
FF_CHUNK = 256
VMEM_LIMIT = 56 * 1024 * 1024

C_QKV = (0, 768)
C_CONV = (768, 2304)
C_ZAB = (2304, 2944)
C_GATE = (2944, 4992)
W_COLS = 4992


def _dot(a, b):
    return jnp.dot(a, b, preferred_element_type=F32)


def _dot_nt(a, b):
    return lax.dot_general(a, b, (((1,), (1,)), ((), ())), preferred_element_type=F32)


def _sigmoid(x):
    return 1.0 / (1.0 + jnp.exp(-x))


def _rms(x, w):
    return x * lax.rsqrt(jnp.mean(x * x, axis=-1, keepdims=True) + EPS) * w


def _proj_kernel(x_ref, nw_ref, w_ref, cos_ref, sin_ref,
                 q_ref, k_ref, v_ref, conv_ref, z_ref, ab_ref, gate_ref):
    xn = _rms(x_ref[...], nw_ref[...]).astype(BF16)
    cos = cos_ref[...]
    sin = sin_ref[...]
    lane = lax.broadcasted_iota(jnp.int32, cos.shape, 1)
    first_half = (lane & (HEAD_DIM - 1)) < (HEAD_DIM // 2)

    def rope(t):
        rot = jnp.where(first_half, pltpu.roll(t, LANES - HEAD_DIM // 2, 1), pltpu.roll(t, HEAD_DIM // 2, 1))
        return t * cos + rot * sin

    qkv = _dot(xn, w_ref[:, C_QKV[0]:C_QKV[1]])
    for j in range(Q_A // LANES):
        q_ref[:, j * LANES:(j + 1) * LANES] = rope(qkv[:, j * LANES:(j + 1) * LANES]) * (HEAD_DIM ** -0.5)
    k_ref[...] = rope(qkv[:, Q_A:Q_A + KV_A])
    v_ref[...] = qkv[:, Q_A + KV_A:Q_A + 2 * KV_A]
    conv_ref[...] = _dot(xn, w_ref[:, C_CONV[0]:C_CONV[1]])
    zab = _dot(xn, w_ref[:, C_ZAB[0]:C_ZAB[1]])
    z_ref[...] = zab[:, :V_B]
    ab_ref[...] = zab[:, V_B:]
    gate_ref[...] = _sigmoid(_dot(xn, w_ref[:, C_GATE[0]:C_GATE[1]]))


def _proj(x, norm_w, w_packed, cos_t, sin_t, tm):
    nb, t, _ = x.shape
    assert t % tm == 0
    grid = (nb, t // tm)

    def row(width):
        return pl.BlockSpec((None, tm, width), lambda b, i: (b, i, 0))

    widths = (Q_A, KV_A, KV_A, CONV_DIM, V_B, LANES, 2 * D_MODEL)
    return pl.pallas_call(
        _proj_kernel,
        grid=grid,
        in_specs=[
            row(D_MODEL),
            pl.BlockSpec((1, D_MODEL), lambda b, i: (0, 0)),
            pl.BlockSpec((D_MODEL, W_COLS), lambda b, i: (0, 0), pipeline_mode=pl.Buffered(1)),
            pl.BlockSpec((tm, LANES), lambda b, i: (i, 0)),
            pl.BlockSpec((tm, LANES), lambda b, i: (i, 0)),
        ],
        out_specs=[row(w) for w in widths],
        out_shape=[jax.ShapeDtypeStruct((nb, t, w), F32) for w in widths],
        compiler_params=pltpu.CompilerParams(
            dimension_semantics=("arbitrary", "arbitrary"), vmem_limit_bytes=VMEM_LIMIT),
        name="proj",
    )(x, norm_w, w_packed, cos_t, sin_t)


def _attn_kernel(sinks_ref, q_ref, kp_ref, vp_ref, kc_ref, vc_ref, *out_refs, sb, tq, first_valid, n_new):
    o_ref = out_refs[0]
    pv = jnp.where(pl.program_id(1) == 0, first_valid, 0)
    lane = lax.broadcasted_iota(jnp.int32, (BLK, LANES), 1)
    lo_k = lane < HEAD_DIM
    lo_q = lane[:tq] < HEAD_DIM
    rows = lax.broadcasted_iota(jnp.int32, (2 * tq, BLK), 0)
    qi = jnp.where(rows < tq, rows, rows - tq)
    kj = lax.broadcasted_iota(jnp.int32, (2 * tq, BLK), 1)
    mask_p = (kj >= qi) & (kj >= pv)
    mask_c = kj <= qi
    row1 = lax.broadcasted_iota(jnp.int32, (2 * tq, 1), 0)

    def pad_rows(x):
        if tq == BLK:
            return x
        return jnp.concatenate([x, jnp.zeros((BLK - tq, LANES), F32)], axis=0)

    def dup(x, kv):
        r = pltpu.roll(x, HEAD_DIM, 1)
        return (jnp.where(lo_k, x, r) if kv == 0 else jnp.where(lo_k, r, x)).astype(BF16)

    for s in range(sb):
        kp, vp = kp_ref[s], vp_ref[s]
        kc, vc = pad_rows(kc_ref[s]), pad_rows(vc_ref[s])
        for kv in range(N_KV_HEADS):
            kdp, vdp, kdc, vdc = dup(kp, kv), dup(vp, kv), dup(kc, kv), dup(vc, kv)
            for jj in range(2):
                j = 2 * kv + jj
                q2 = q_ref[s, :, j * LANES:(j + 1) * LANES]
                qs = jnp.concatenate([jnp.where(lo_q, q2, 0.0), jnp.where(lo_q, 0.0, q2)], axis=0).astype(BF16)
                sp = jnp.where(mask_p, _dot_nt(qs, kdp), -jnp.inf)
                sc = jnp.where(mask_c, _dot_nt(qs, kdc), -jnp.inf)
                sk = jnp.where(row1 < tq, sinks_ref[2 * j], sinks_ref[2 * j + 1])
                m = jnp.maximum(jnp.maximum(jnp.max(sp, axis=-1, keepdims=True),
                                            jnp.max(sc, axis=-1, keepdims=True)), sk)
                pp = jnp.exp(sp - m)
                pc = jnp.exp(sc - m)
                den = (jnp.sum(pp, axis=-1, keepdims=True) + jnp.sum(pc, axis=-1, keepdims=True)
                       + jnp.exp(sk - m))
                o = (_dot(pp.astype(BF16), vdp) + _dot(pc.astype(BF16), vdc)) / den
                o_ref[s, :, j * LANES:(j + 1) * LANES] = jnp.where(lo_q, o[:tq], o[tq:])
        if n_new:
            ko_ref, vo_ref = out_refs[1], out_refs[2]
            ko_ref[s, 0:BLK - n_new, :] = kp_ref[s, n_new:BLK, :]
            ko_ref[s, BLK - n_new:BLK, :] = kc_ref[s, 0:n_new, :]
            vo_ref[s, 0:BLK - n_new, :] = vp_ref[s, n_new:BLK, :]
            vo_ref[s, BLK - n_new:BLK, :] = vc_ref[s, 0:n_new, :]


def _attn_prompt(sinks, q, k, v, n_real_blocks):
    nb = q.shape[0]
    last = n_real_blocks

    def cur(b, i):
        return (b, i, 0)

    def prev(b, i):
        return (b, jnp.where(i == 0, last, i - 1), 0)

    kern = functools.partial(_attn_kernel, sb=1, tq=BLK, first_valid=PAD_ROWS, n_new=0)
    return pl.pallas_call(
        kern,
        grid=(nb, n_real_blocks),
        in_specs=[
            pl.BlockSpec(memory_space=pltpu.SMEM),
            pl.BlockSpec((1, BLK, Q_A), cur),
            pl.BlockSpec((1, BLK, KV_A), prev),
            pl.BlockSpec((1, BLK, KV_A), prev),
            pl.BlockSpec((1, BLK, KV_A), cur),
            pl.BlockSpec((1, BLK, KV_A), cur),
        ],
        out_specs=pl.BlockSpec((1, BLK, Q_A), cur),
        out_shape=jax.ShapeDtypeStruct((nb, n_real_blocks * BLK, Q_A), F32),
        compiler_params=pltpu.CompilerParams(
            dimension_semantics=("arbitrary", "arbitrary"), vmem_limit_bytes=VMEM_LIMIT),
        name="attn_prompt",
    )(sinks, q, k, v, k, v)


def _attn_sample(sinks, q, k_new, v_new, k_past, v_past, n_new, sb):
    ns, tq, _ = q.shape
    assert ns % sb == 0

    def idx(g, i):
        return (g, 0, 0)

    kern = functools.partial(_attn_kernel, sb=sb, tq=tq, first_valid=0, n_new=n_new)
    cache = jax.ShapeDtypeStruct((ns, BLK, KV_A), F32)
    return pl.pallas_call(
        kern,
        grid=(ns // sb, 1),
        in_specs=[
            pl.BlockSpec(memory_space=pltpu.SMEM),
            pl.BlockSpec((sb, tq, Q_A), idx),
            pl.BlockSpec((sb, BLK, KV_A), idx),
            pl.BlockSpec((sb, BLK, KV_A), idx),
            pl.BlockSpec((sb, tq, KV_A), idx),
            pl.BlockSpec((sb, tq, KV_A), idx),
        ],
        out_specs=[pl.BlockSpec((sb, tq, Q_A), idx), pl.BlockSpec((sb, BLK, KV_A), idx),
                   pl.BlockSpec((sb, BLK, KV_A), idx)],
        out_shape=[jax.ShapeDtypeStruct((ns, tq, Q_A), F32), cache, cache],
        compiler_params=pltpu.CompilerParams(
            dimension_semantics=("arbitrary", "arbitrary"), vmem_limit_bytes=VMEM_LIMIT),
        name="attn_sample",
    )(sinks, q, k_past, v_past, k_new, v_new)


def _split_bf16(a):
    hi = a.astype(BF16)
    return hi, (a - hi.astype(F32)).astype(BF16)


def _mm(a, b, passes, nt=False):
    f = _dot_nt if nt else _dot
    if passes == 1:
        return f(a.astype(BF16), b.astype(BF16))
    ah, al = _split_bf16(a)
    bh, bl = _split_bf16(b)
    return f(al, bh) + f(ah, bl) + f(ah, bh)


def _unit_lower_inverse(n_mat, r, c, levels, passes):
    eye = (r == c).astype(F32)
    x = eye - jnp.where(((r ^ c) == 1) & (r > c), n_mat, 0.0)
    for lg in range(1, levels):
        same_2g = (r >> (lg + 1)) == (c >> (lg + 1))
        joins = same_2g & (((r >> lg) & 1) == 1) & (((c >> lg) & 1) == 0)
        e = jnp.where(joins, n_mat, 0.0)
        x = x - _mm(_mm(x, e, passes), x, passes)
    return x


def _delta_kernel(conv_ref, ab_ref, z_ref, tail0_ref, s0_ref, convw_ref, alog_ref, dtb_ref, gdnw_ref,
                  o_ref, sout_ref, tail_sc, s_sc, *, tb, n_valid, levels, inv_passes):
    i = pl.program_id(1)

    @pl.when(i == 0)
    def _():
        tail_sc[...] = tail0_ref[...]
        s_sc[...] = s0_ref[...]

    def pad_rows(x):
        if tb == BLK:
            return x
        return jnp.concatenate([x, jnp.zeros((BLK - tb, x.shape[1]), F32)], axis=0)

    u = pad_rows(conv_ref[...])
    ext = jnp.concatenate([tail_sc[...], u], axis=0)
    tail_sc[...] = u[BLK - 8:BLK]
    y = ext[5:5 + BLK] * convw_ref[0:1, :]
    y = y + ext[6:6 + BLK] * convw_ref[1:2, :]
    y = y + ext[7:7 + BLK] * convw_ref[2:3, :]
    y = y + u * convw_ref[3:4, :]
    cb = y * _sigmoid(y)

    r = lax.broadcasted_iota(jnp.int32, (BLK, BLK), 0)
    c = lax.broadcasted_iota(jnp.int32, (BLK, BLK), 1)
    incl = r >= c
    strict = r > c
    row_ok = r < n_valid

    ab = pad_rows(ab_ref[...])
    xg = ab + dtb_ref[...]
    softplus = jnp.maximum(xg, 0.0) + jnp.log1p(jnp.exp(-jnp.abs(xg)))
    gb = jnp.where(c < N_HEADS_B, -jnp.exp(alog_ref[...]) * softplus, _sigmoid(ab))
    gb = jnp.where(row_ok, gb, 0.0)
    g_rows = gb.T[0:8]
    lane8 = c[0:8]
    gc_rows = g_rows
    sh = 1
    while sh < BLK:
        gc_rows = gc_rows + jnp.where(lane8 >= sh, pltpu.roll(gc_rows, sh, 1), 0.0)
        sh *= 2
    gc_cols = jnp.concatenate([gc_rows, jnp.zeros((BLK - 8, BLK), F32)], axis=0).T

    for h in range(N_HEADS_B):
        sl = slice(h * DK, (h + 1) * DK)
        qh = cb[:, h * DK:(h + 1) * DK]
        kh = cb[:, QK_B + h * DK:QK_B + (h + 1) * DK]
        vh = cb[:, 2 * QK_B + h * DK:2 * QK_B + (h + 1) * DK]
        qh = qh * lax.rsqrt(jnp.sum(qh * qh, axis=-1, keepdims=True) + EPS) * (DK ** -0.5)
        kh = kh * lax.rsqrt(jnp.sum(kh * kh, axis=-1, keepdims=True) + EPS)
        if n_valid < BLK:
            qh = jnp.where(row_ok, qh, 0.0)
            kh = jnp.where(row_ok, kh, 0.0)
        beta = gb[:, N_HEADS_B + h:N_HEADS_B + h + 1]
        gcc = gc_cols[:, h:h + 1]
        gcr = gc_rows[h:h + 1, :]
        gc_last = gc_rows[h:h + 1, BLK - 1:BLK]
        decay = jnp.where(incl, jnp.exp(jnp.where(incl, gcc - gcr, 0.0)), 0.0)
        eg = jnp.exp(gcc)
        n_mat = jnp.where(strict, beta * decay * _mm(kh, kh, 1, nt=True), 0.0)
        t_inv = _unit_lower_inverse(n_mat, r, c, levels, inv_passes)
        rhs = jnp.concatenate([beta * eg * kh, beta * vh], axis=1)
        wu = _mm(t_inv, rhs, inv_passes)
        w_mat, u0 = wu[:, :DK], wu[:, DK:]
        qk = _mm(qh, kh, 1, nt=True) * decay
        s_old = s_sc[h]
        ws_qs = _mm(jnp.concatenate([w_mat, qh * eg], axis=0), s_old, 1)
        u_new = u0 - ws_qs[:BLK]
        o = ws_qs[BLK:] + _mm(qk, u_new, 1)
        k_dec = kh * jnp.exp(gc_last - gcc)
        s_sc[h] = jnp.exp(gc_last) * s_old + _mm(k_dec.T, u_new, 1)
        zh = pad_rows(z_ref[:, sl])
        og = _rms(o, gdnw_ref[...]) * (zh * _sigmoid(zh))
        o_ref[:, sl] = og[:tb]

    @pl.when(i == pl.num_programs(1) - 1)
    def _():
        sout_ref[...] = s_sc[...]


def _delta(conv_in, ab, z, tail0, s0, conv_w, alog, dtb, gdnw, *, tb, n_valid, n_blocks, block_of, out_rows,
           out_block_of, inv_passes):
    ns = conv_in.shape[0]
    levels = max(1, (n_valid - 1).bit_length())

    def inp(width):
        return pl.BlockSpec((None, tb, width), lambda b, i: (b, block_of(i), 0))

    def const(shape):
        return pl.BlockSpec(shape, lambda b, i: (0,) * len(shape))

    kern = functools.partial(_delta_kernel, tb=tb, n_valid=n_valid, levels=levels, inv_passes=inv_passes)
    return pl.pallas_call(
        kern,
        grid=(ns, n_blocks),
        in_specs=[
            inp(CONV_DIM), inp(LANES), inp(V_B),
            pl.BlockSpec((None, 8, CONV_DIM), lambda b, i: (b, 0, 0)),
            pl.BlockSpec((None, N_HEADS_B, DK, DK), lambda b, i: (b, 0, 0, 0)),
            const((CONV_W, CONV_DIM)), const((1, LANES)), const((1, LANES)), const((1, DK)),
        ],
        out_specs=[
            pl.BlockSpec((None, tb, V_B), lambda b, i: (b, out_block_of(i), 0)),
            pl.BlockSpec((None, N_HEADS_B, DK, DK), lambda b, i: (b, 0, 0, 0)),
        ],
        out_shape=[jax.ShapeDtypeStruct((ns, out_rows, V_B), F32),
                   jax.ShapeDtypeStruct((ns, N_HEADS_B, DK, DK), F32)],
        scratch_shapes=[pltpu.VMEM((8, CONV_DIM), F32), pltpu.VMEM((N_HEADS_B, DK, DK), F32)],
        compiler_params=pltpu.CompilerParams(
            dimension_semantics=("arbitrary", "arbitrary"), vmem_limit_bytes=VMEM_LIMIT),
        name="delta",
    )(conv_in, ab, z, tail0, s0, conv_w, alog, dtb, gdnw)


def _ffn_kernel(x_ref, oa_ref, ob_ref, g_ref, woa_ref, wob_ref, wout_ref, nffn_ref, wg_ref, wu_ref, wd_ref,
                nfin_ref, y_ref, acc_ref):
    ga = g_ref[:, :D_MODEL]
    gb = g_ref[:, D_MODEL:]
    merged = ga * _dot(oa_ref[...].astype(BF16), woa_ref[...]) + gb * _dot(ob_ref[...].astype(BF16), wob_ref[...])
    h = x_ref[...] + _dot(merged.astype(BF16), wout_ref[...])
    hn = _rms(h, nffn_ref[...]).astype(BF16)
    acc_ref[...] = h
    for c0 in range(0, D_FF, FF_CHUNK):
        gate = _dot(hn, wg_ref[:, c0:c0 + FF_CHUNK])
        up = _dot(hn, wu_ref[:, c0:c0 + FF_CHUNK])
        act = (gate * _sigmoid(gate) * up).astype(BF16)
        acc_ref[...] += _dot(act, wd_ref[c0:c0 + FF_CHUNK, :])
    y_ref[...] = _rms(acc_ref[...], nfin_ref[...])


def _ffn(x, oa, ob, gates, w_oa, w_ob, w_out, norm_ffn, w_gate, w_up, w_down, norm_final, tm):
    nb, t, _ = x.shape
    assert t % tm == 0

    def row(width):
        return pl.BlockSpec((None, tm, width), lambda b, i: (b, i, 0))

    def const(shape):
        return pl.BlockSpec(shape, lambda b, i: (0, 0), pipeline_mode=pl.Buffered(1))

    return pl.pallas_call(
        _ffn_kernel,
        grid=(nb, t // tm),
        in_specs=[
            row(D_MODEL), row(Q_A), row(V_B), row(2 * D_MODEL),
            const((Q_A, D_MODEL)), const((V_B, D_MODEL)), const((D_MODEL, D_MODEL)), const((1, D_MODEL)),
            const((D_MODEL, D_FF)), const((D_MODEL, D_FF)), const((D_FF, D_MODEL)), const((1, D_MODEL)),
        ],
        out_specs=row(D_MODEL),
        out_shape=jax.ShapeDtypeStruct((nb, t, D_MODEL), F32),
        scratch_shapes=[pltpu.VMEM((tm, D_MODEL), F32)],
        compiler_params=pltpu.CompilerParams(
            dimension_semantics=("arbitrary", "arbitrary"), vmem_limit_bytes=VMEM_LIMIT),
        name="ffn",
    )(x, oa, ob, gates, w_oa, w_ob, w_out, norm_ffn, w_gate, w_up, w_down, norm_final)


def _rope_tables(pos):
    half = HEAD_DIM // 2
    inv = ROPE_THETA ** (-jnp.arange(half, dtype=F32) / half)
    ang = pos.astype(F32)[:, None] * inv[None, :]
    cos, sin = jnp.cos(ang), jnp.sin(ang)
    reps = LANES // HEAD_DIM
    return (jnp.tile(jnp.concatenate([cos, cos], axis=1), (1, reps)),
            jnp.tile(jnp.concatenate([-sin, sin], axis=1), (1, reps)))


def _pad_seq_rows(x, rows, front=0):
    return jnp.pad(x, ((0, 0), (front, rows - front - x.shape[1]), (0, 0)))


def kernel(x_prompt, x_sample, cache_swa_k, cache_swa_v, state_delta, state_conv, meta_tokens, norm_mix, w_in,
           conv_w, a_log, dt_bias, sinks, gdn_norm, w_oa, w_ob, w_out, norm_ffn, w_gate, w_up, w_down,
           norm_final):
    assert w_in.shape[0] == 1, "single trunk layer"
    nb, seq, _ = x_prompt.shape
    ns, n_new, _ = x_sample.shape
    assert seq % BLK == 0 and n_new <= 8
    n_real = seq // BLK
    t_pad = seq + BLK

    w = w_in[0]
    c_ab = Q_A + 2 * KV_A + CONV_DIM + V_B
    w_packed = jnp.concatenate(
        [w[:, :c_ab], jnp.pad(w[:, c_ab:c_ab + 2 * N_HEADS_B], ((0, 0), (0, LANES - 2 * N_HEADS_B))),
         w[:, c_ab + 2 * N_HEADS_B:]], axis=1).astype(BF16)
    nmix = norm_mix[0][None, :]
    lane_pad = (0, LANES - N_HEADS_B)
    alog = jnp.pad(a_log[0], lane_pad)[None, :]
    dtb = jnp.pad(dt_bias[0], lane_pad)[None, :]
    gdnw = gdn_norm[0][None, :]
    ffn_w = (w_oa[0].astype(BF16), w_ob[0].astype(BF16), w_out[0].astype(BF16), norm_ffn[0][None, :],
             w_gate[0].astype(BF16), w_up[0].astype(BF16), w_down[0].astype(BF16), norm_final[None, :])
    sk = sinks[0]

    meta = jnp.broadcast_to(meta_tokens.astype(F32)[None], (nb, N_META, D_MODEL))
    x_pad = jnp.concatenate([x_prompt, jnp.zeros((nb, PAD_ROWS, D_MODEL), F32), meta], axis=1)
    pos_p = jnp.concatenate([N_META + jnp.arange(seq, dtype=jnp.int32),
                             jnp.arange(BLK, dtype=jnp.int32) - PAD_ROWS])
    cos_p, sin_p = _rope_tables(pos_p)
    q_p, k_p, v_p, conv_p, z_p, ab_p, gate_p = _proj(x_pad, nmix, w_packed, cos_p, sin_p, tm=3 * BLK)
    oa_p = _attn_prompt(sk, q_p, k_p, v_p, n_real)
    ob_p, s_p = _delta(
        conv_p, ab_p, z_p, jnp.zeros((nb, 8, CONV_DIM), F32), jnp.zeros((nb, N_HEADS_B, DK, DK), F32),
        conv_w[0], alog, dtb, gdnw, tb=BLK, n_valid=BLK, n_blocks=n_real + 1,
        block_of=lambda i: jnp.where(i == 0, n_real, i - 1), out_rows=seq,
        out_block_of=lambda i: jnp.maximum(i - 1, 0), inv_passes=3)
    y_prompt = _ffn(x_prompt, oa_p, ob_p, gate_p, *ffn_w, tm=2 * BLK)

    rows_s = ns * n_new
    pos_s = PAST_LEN + (jnp.arange(rows_s, dtype=jnp.int32) % n_new)
    cos_s, sin_s = _rope_tables(pos_s)
    outs = _proj(x_sample.reshape(1, rows_s, D_MODEL), nmix, w_packed, cos_s, sin_s, tm=rows_s // 2)
    q_s, k_s, v_s, conv_s, z_s, ab_s, gate_s = [o.reshape(ns, n_new, o.shape[-1]) for o in outs]
    pad8 = functools.partial(_pad_seq_rows, rows=8)
    oa_s, k_cache, v_cache = _attn_sample(
        sk, pad8(q_s), pad8(k_s), pad8(v_s), cache_swa_k[0].reshape(ns, BLK, KV_A),
        cache_swa_v[0].reshape(ns, BLK, KV_A), n_new, sb=8)
    tail_s = _pad_seq_rows(state_conv[0], 8, front=8 - (CONV_W - 1))
    ob_s, s_s = _delta(
        pad8(conv_s), pad8(ab_s), pad8(z_s), tail_s, state_delta[0], conv_w[0], alog, dtb, gdnw,
        tb=8, n_valid=n_new, n_blocks=1, block_of=lambda i: i, out_rows=8, out_block_of=lambda i: i,
        inv_passes=3)
    y_sample = _ffn(x_sample.reshape(1, rows_s, D_MODEL), oa_s[:, :n_new].reshape(1, rows_s, Q_A),
                    ob_s[:, :n_new].reshape(1, rows_s, V_B), gate_s.reshape(1, rows_s, 2 * D_MODEL),
                    *ffn_w, tm=rows_s // 2).reshape(ns, n_new, D_MODEL)

    kv_shape = (1, -1, BLK, N_KV_HEADS, HEAD_DIM)
    conv_new_s = jnp.concatenate([state_conv[0], conv_s], axis=1)[:, -(CONV_W - 1):]
    return (y_prompt, y_sample,
            k_p[:, seq - BLK:seq].reshape(kv_shape), v_p[:, seq - BLK:seq].reshape(kv_shape),
            s_p[None], conv_p[:, seq - (CONV_W - 1):seq][None],
            k_cache.reshape(kv_shape), v_cache.reshape(kv_shape), s_s[None], conv_new_s[None])
```

```python
import functools

import jax
import jax.numpy as jnp
from jax import lax
from jax.experimental import pallas as pl
from jax.experimental.pallas import tpu as pltpu

F32 = jnp.float32
BF16 = jnp.bfloat16

D_MODEL = 1024
N_META = 16
EPS = 1e-6
HEAD_DIM = 64
N_HEADS_A = 8
N_KV_HEADS = 2
BLK = 128
ROPE_THETA = 10000.0
PAST_LEN = 16384
DK = 128
N_HEADS_B = 4
CONV_W = 4
D_FF = 2816
Q_A = N_HEADS_A * HEAD_DIM
KV_A = N_KV_HEADS * HEAD_DIM
QK_B = N_HEADS_B * DK
V_B = N_HEADS_B * DK
CONV_DIM = 2 * QK_B + V_B
LANES = 128
SUBLANES = 8
PAD_ROWS = BLK - N_META
FF_CHUNK = 256
VMEM_LIMIT = 56 * 1024 * 1024

C_QKV = (0, 768)
C_CONV = (768, 2304)
C_ZAB = (2304, 2944)
C_GATE = (2944, 4992)
W_COLS = 4992

_ARB2 = pltpu.CompilerParams(dimension_semantics=("arbitrary", "arbitrary"), vmem_limit_bytes=VMEM_LIMIT)
_ARB1 = pltpu.CompilerParams(dimension_semantics=("arbitrary",), vmem_limit_bytes=VMEM_LIMIT)


def _dot(a, b):
    return jnp.dot(a, b, preferred_element_type=F32)


def _dot_nt(a, b):
    return lax.dot_general(a, b, (((1,), (1,)), ((), ())), preferred_element_type=F32)


def _bmm(a, b):
    return lax.dot_general(a, b, (((2,), (1,)), ((0,), (0,))), preferred_element_type=F32)


def _sigmoid(x):
    return 1.0 / (1.0 + jnp.exp(-x))


def _rms(x, w):
    return x * lax.rsqrt(jnp.mean(x * x, axis=-1, keepdims=True) + EPS) * w


def _proj_kernel(x_ref, nw_ref, w_ref, cos_ref, sin_ref,
                 q_ref, k_ref, v_ref, conv_ref, z_ref, ab_ref, gate_ref):
    xn = _rms(x_ref[...], nw_ref[...]).astype(BF16)
    cos = cos_ref[...]
    sin = sin_ref[...]
    lane = lax.broadcasted_iota(jnp.int32, cos.shape, 1)
    first_half = (lane & (HEAD_DIM - 1)) < (HEAD_DIM // 2)

    def rope(t):
        rot = jnp.where(first_half, pltpu.roll(t, LANES - HEAD_DIM // 2, 1), pltpu.roll(t, HEAD_DIM // 2, 1))
        return t * cos + rot * sin

    qkv = _dot(xn, w_ref[:, C_QKV[0]:C_QKV[1]])
    for j in range(Q_A // LANES):
        q_ref[:, j * LANES:(j + 1) * LANES] = rope(qkv[:, j * LANES:(j + 1) * LANES]) * (HEAD_DIM ** -0.5)
    k_ref[...] = rope(qkv[:, Q_A:Q_A + KV_A])
    v_ref[...] = qkv[:, Q_A + KV_A:Q_A + 2 * KV_A]
    conv_ref[...] = _dot(xn, w_ref[:, C_CONV[0]:C_CONV[1]])
    zab = _dot(xn, w_ref[:, C_ZAB[0]:C_ZAB[1]])
    z_ref[...] = zab[:, :V_B]
    ab_ref[...] = zab[:, V_B:]
    gate_ref[...] = _sigmoid(_dot(xn, w_ref[:, C_GATE[0]:C_GATE[1]]))


def _proj(x, norm_w, w_packed, cos_t, sin_t, tm):
    nb, t, _ = x.shape
    assert t % tm == 0

    def row(width):
        return pl.BlockSpec((None, tm, width), lambda b, i: (b, i, 0))

    widths = (Q_A, KV_A, KV_A, CONV_DIM, V_B, LANES, 2 * D_MODEL)
    return pl.pallas_call(
        _proj_kernel,
        grid=(nb, t // tm),
        in_specs=[
            row(D_MODEL),
            pl.BlockSpec((1, D_MODEL), lambda b, i: (0, 0)),
            pl.BlockSpec((D_MODEL, W_COLS), lambda b, i: (0, 0), pipeline_mode=pl.Buffered(1)),
            pl.BlockSpec((tm, LANES), lambda b, i: (i, 0)),
            pl.BlockSpec((tm, LANES), lambda b, i: (i, 0)),
        ],
        out_specs=[row(w) for w in widths],
        out_shape=[jax.ShapeDtypeStruct((nb, t, w), F32) for w in widths],
        compiler_params=_ARB2,
        name="proj",
    )(x, norm_w, w_packed, cos_t, sin_t)


def _attn_kernel(sinks_ref, q_ref, kp_ref, vp_ref, kc_ref, vc_ref, *rest, sb, tq, has_meta, n_new):
    if has_meta:
        km_ref, vm_ref, o_ref = rest
    else:
        o_ref = rest[0]
    first = pl.program_id(1) == 0
    pv = jnp.where(first, PAD_ROWS, 0) if has_meta else 0
    lane = lax.broadcasted_iota(jnp.int32, (BLK, LANES), 1)
    lo_k = lane < HEAD_DIM
    lo_q = lane[:tq] < HEAD_DIM
    rows = lax.broadcasted_iota(jnp.int32, (2 * tq, BLK), 0)
    qi = jnp.where(rows < tq, rows, rows - tq)
    kj = lax.broadcasted_iota(jnp.int32, (2 * tq, BLK), 1)
    mask_p = (kj >= qi) & (kj >= pv)
    mask_c = kj <= qi
    row1 = lax.broadcasted_iota(jnp.int32, (2 * tq, 1), 0)

    def pad_rows(x):
        if tq == BLK:
            return x
        return jnp.concatenate([x, jnp.zeros((BLK - tq, LANES), F32)], axis=0)

    def dup(x, kv):
        r = pltpu.roll(x, HEAD_DIM, 1)
        return (jnp.where(lo_k, x, r) if kv == 0 else jnp.where(lo_k, r, x)).astype(BF16)

    for s in range(sb):
        kp, vp = kp_ref[s], vp_ref[s]
        if has_meta:
            kp = jnp.where(first, km_ref[...], kp)
            vp = jnp.where(first, vm_ref[...], vp)
        kc, vc = pad_rows(kc_ref[s]), pad_rows(vc_ref[s])
        for kv in range(N_KV_HEADS):
            kdp, vdp, kdc, vdc = dup(kp, kv), dup(vp, kv), dup(kc, kv), dup(vc, kv)
            for jj in range(2):
                j = 2 * kv + jj
                q2 = q_ref[s, :, j * LANES:(j + 1) * LANES]
                qs = jnp.concatenate([jnp.where(lo_q, q2, 0.0), jnp.where(lo_q, 0.0, q2)], axis=0).astype(BF16)
                sp = jnp.where(mask_p, _dot_nt(qs, kdp), -jnp.inf)
                sc = jnp.where(mask_c, _dot_nt(qs, kdc), -jnp.inf)
                sk = jnp.where(row1 < tq, sinks_ref[2 * j], sinks_ref[2 * j + 1])
                m = jnp.maximum(jnp.maximum(jnp.max(sp, axis=-1, keepdims=True),
                                            jnp.max(sc, axis=-1, keepdims=True)), sk)
                pp = jnp.exp(sp - m)
                pc = jnp.exp(sc - m)
                den = (jnp.sum(pp, axis=-1, keepdims=True) + jnp.sum(pc, axis=-1, keepdims=True)
                       + jnp.exp(sk - m))
                o = (_dot(pp.astype(BF16), vdp) + _dot(pc.astype(BF16), vdc)) / den
                o_ref[s, :, j * LANES:(j + 1) * LANES] = jnp.where(lo_q, o[:tq], o[tq:])
        if n_new:
            ko_ref, vo_ref = rest[1], rest[2]
            ko_ref[s, 0:BLK - n_new, :] = kp_ref[s, n_new:BLK, :]
            ko_ref[s, BLK - n_new:BLK, :] = kc_ref[s, 0:n_new, :]
            vo_ref[s, 0:BLK - n_new, :] = vp_ref[s, n_new:BLK, :]
            vo_ref[s, BLK - n_new:BLK, :] = vc_ref[s, 0:n_new, :]


def _attn_prompt(sinks, q, k, v, k_meta, v_meta):
    nb, t, _ = q.shape

    def cur(b, i):
        return (b, i, 0)

    def prev(b, i):
        return (b, jnp.maximum(i - 1, 0), 0)

    kern = functools.partial(_attn_kernel, sb=1, tq=BLK, has_meta=True, n_new=0)
    meta_spec = pl.BlockSpec((BLK, KV_A), lambda b, i: (0, 0))
    return pl.pallas_call(
        kern,
        grid=(nb, t // BLK),
        in_specs=[
            pl.BlockSpec(memory_space=pltpu.SMEM),
            pl.BlockSpec((1, BLK, Q_A), cur),
            pl.BlockSpec((1, BLK, KV_A), prev),
            pl.BlockSpec((1, BLK, KV_A), prev),
            pl.BlockSpec((1, BLK, KV_A), cur),
            pl.BlockSpec((1, BLK, KV_A), cur),
            meta_spec, meta_spec,
        ],
        out_specs=pl.BlockSpec((1, BLK, Q_A), cur),
        out_shape=jax.ShapeDtypeStruct((nb, t, Q_A), F32),
        compiler_params=_ARB2,
        name="attn_prompt",
    )(sinks, q, k, v, k, v, k_meta, v_meta)


def _attn_sample(sinks, q, k_new, v_new, k_past, v_past, n_new, sb):
    ns, tq, _ = q.shape
    assert ns % sb == 0

    def idx(g, i):
        return (g, 0, 0)

    kern = functools.partial(_attn_kernel, sb=sb, tq=tq, has_meta=False, n_new=n_new)
    cache = jax.ShapeDtypeStruct((ns, BLK, KV_A), F32)
    return pl.pallas_call(
        kern,
        grid=(ns // sb, 1),
        in_specs=[
            pl.BlockSpec(memory_space=pltpu.SMEM),
            pl.BlockSpec((sb, tq, Q_A), idx),
            pl.BlockSpec((sb, BLK, KV_A), idx),
            pl.BlockSpec((sb, BLK, KV_A), idx),
            pl.BlockSpec((sb, tq, KV_A), idx),
            pl.BlockSpec((sb, tq, KV_A), idx),
        ],
        out_specs=[pl.BlockSpec((sb, tq, Q_A), idx), pl.BlockSpec((sb, BLK, KV_A), idx),
                   pl.BlockSpec((sb, BLK, KV_A), idx)],
        out_shape=[jax.ShapeDtypeStruct((ns, tq, Q_A), F32), cache, cache],
        compiler_params=_ARB2,
        name="attn_sample",
    )(sinks, q, k_past, v_past, k_new, v_new)


def _unit_lower_inverse(n_mat, r, c, levels):
    batched = n_mat.ndim == 3
    mm = _bmm if batched else _dot

    def lift(mask):
        return mask[None] if batched else mask

    x = lift((r == c).astype(F32)) - jnp.where(lift(((r ^ c) == 1) & (r > c)), n_mat, 0.0)
    for lg in range(1, levels):
        same_2g = (r >> (lg + 1)) == (c >> (lg + 1))
        joins = same_2g & (((r >> lg) & 1) == 1) & (((c >> lg) & 1) == 0)
        e = jnp.where(lift(joins), n_mat, 0.0).astype(BF16)
        xb = x.astype(BF16)
        x = x - mm(mm(xb, e).astype(BF16), xb)
    return x


def _conv_silu(u, tail, convw_ref):
    n = u.shape[0]
    ext = jnp.concatenate([tail, u], axis=0)
    y = ext[5:5 + n] * convw_ref[0:1, :]
    y = y + ext[6:6 + n] * convw_ref[1:2, :]
    y = y + ext[7:7 + n] * convw_ref[2:3, :]
    y = y + u * convw_ref[3:4, :]
    return y * _sigmoid(y)


def _decay_gates(ab, alog_ref, dtb_ref, lane):
    xg = ab + dtb_ref[...]
    softplus = jnp.maximum(xg, 0.0) + jnp.log1p(jnp.exp(-jnp.abs(xg)))
    return jnp.where(lane < N_HEADS_B, -jnp.exp(alog_ref[...]) * softplus, _sigmoid(ab))


def _prefix_sum_lanes(x, lane):
    sh = 1
    while sh < x.shape[-1]:
        x = x + jnp.where(lane >= sh, pltpu.roll(x, sh, 1), 0.0)
        sh *= 2
    return x


def _delta_prep_kernel(conv_ref, tailp_ref, tailf_ref, ab_ref, convw_ref, alog_ref, dtb_ref,
                       wq_ref, u0_ref, qk_ref, kdt_ref, cdec_ref, n_sc, rhs_sc, *, g_blocks):
    rows = g_blocks * BLK
    tail = jnp.where(pl.program_id(1) == 0, tailf_ref[...], tailp_ref[...])
    cb = _conv_silu(conv_ref[...], tail, convw_ref)
    lane = lax.broadcasted_iota(jnp.int32, (rows, LANES), 1)
    gb = _decay_gates(ab_ref[...], alog_ref, dtb_ref, lane)

    r = lax.broadcasted_iota(jnp.int32, (BLK, BLK), 0)
    c = lax.broadcasted_iota(jnp.int32, (BLK, BLK), 1)
    incl = r >= c
    strict = r > c
    for j in range(g_blocks):
        rs = slice(j * BLK, (j + 1) * BLK)
        gbj = gb[rs]
        gc_rows = _prefix_sum_lanes(gbj.T[0:SUBLANES], c[0:SUBLANES])
        gc_cols = jnp.concatenate([gc_rows, jnp.zeros((BLK - SUBLANES, BLK), F32)], axis=0).T
        cdec_ref[j] = jnp.exp(jnp.broadcast_to(gc_rows[:, BLK - 1:BLK], (SUBLANES, BLK)))
        for h in range(N_HEADS_B):
            qh = cb[rs, h * DK:(h + 1) * DK]
            kh = cb[rs, QK_B + h * DK:QK_B + (h + 1) * DK]
            vh = cb[rs, 2 * QK_B + h * DK:2 * QK_B + (h + 1) * DK]
            qh = qh * lax.rsqrt(jnp.sum(qh * qh, axis=-1, keepdims=True) + EPS) * (DK ** -0.5)
            kh = kh * lax.rsqrt(jnp.sum(kh * kh, axis=-1, keepdims=True) + EPS)
            beta = gbj[:, N_HEADS_B + h:N_HEADS_B + h + 1]
            gcc = gc_cols[:, h:h + 1]
            gcr = gc_rows[h:h + 1, :]
            gc_last = gc_rows[h:h + 1, BLK - 1:BLK]
            decay = jnp.where(incl, jnp.exp(jnp.where(incl, gcc - gcr, 0.0)), 0.0)
            eg = jnp.exp(gcc)
            kb = kh.astype(BF16)
            n_sc[j * N_HEADS_B + h] = jnp.where(strict, beta * decay * _dot_nt(kb, kb), 0.0)
            rhs_sc[j * N_HEADS_B + h] = jnp.concatenate([beta * eg * kh, beta * vh], axis=1).astype(BF16)
            qk_ref[j, h] = (_dot_nt(qh.astype(BF16), kb) * decay).astype(BF16)
            wq_ref[j, h, BLK:2 * BLK, :] = (qh * eg).astype(BF16)
            kdt_ref[j, h] = (kh * jnp.exp(gc_last - gcc)).T.astype(BF16)

    t_inv = _unit_lower_inverse(n_sc[...], r, c, levels=BLK.bit_length() - 1)
    wu = _bmm(t_inv.astype(BF16), rhs_sc[...])
    for j in range(g_blocks):
        for h in range(N_HEADS_B):
            wq_ref[j, h, 0:BLK, :] = wu[j * N_HEADS_B + h, :, :DK].astype(BF16)
            u0_ref[j, h] = wu[j * N_HEADS_B + h, :, DK:]


def _delta_prep(conv_in, ab, tail_first, conv_w, alog, dtb, g_blocks):
    nb, t, _ = conv_in.shape
    n_blk = t // BLK
    assert n_blk % g_blocks == 0
    rows = g_blocks * BLK
    chains = g_blocks * N_HEADS_B

    def const(shape):
        return pl.BlockSpec(shape, lambda b, j: (0,) * len(shape))

    def out(*tail_shape):
        return pl.BlockSpec((None, g_blocks) + tail_shape, lambda b, j: (b, j) + (0,) * len(tail_shape))

    kern = functools.partial(_delta_prep_kernel, g_blocks=g_blocks)
    return pl.pallas_call(
        kern,
        grid=(nb, n_blk // g_blocks),
        in_specs=[
            pl.BlockSpec((None, rows, CONV_DIM), lambda b, j: (b, j, 0)),
            pl.BlockSpec((None, SUBLANES, CONV_DIM),
                         lambda b, j: (b, jnp.maximum(j * (rows // SUBLANES) - 1, 0), 0)),
            const((SUBLANES, CONV_DIM)),
            pl.BlockSpec((None, rows, LANES), lambda b, j: (b, j, 0)),
            const((CONV_W, CONV_DIM)), const((1, LANES)), const((1, LANES)),
        ],
        out_specs=[out(N_HEADS_B, 2 * BLK, DK), out(N_HEADS_B, BLK, DK), out(N_HEADS_B, BLK, BLK),
                   out(N_HEADS_B, DK, BLK), out(SUBLANES, BLK)],
        out_shape=[
            jax.ShapeDtypeStruct((nb, n_blk, N_HEADS_B, 2 * BLK, DK), BF16),
            jax.ShapeDtypeStruct((nb, n_blk, N_HEADS_B, BLK, DK), F32),
            jax.ShapeDtypeStruct((nb, n_blk, N_HEADS_B, BLK, BLK), BF16),
            jax.ShapeDtypeStruct((nb, n_blk, N_HEADS_B, DK, BLK), BF16),
            jax.ShapeDtypeStruct((nb, n_blk, SUBLANES, BLK), F32),
        ],
        scratch_shapes=[pltpu.VMEM((chains, BLK, BLK), F32), pltpu.VMEM((chains, BLK, 2 * DK), BF16)],
        compiler_params=_ARB2,
        name="delta_prep",
    )(conv_in, conv_in, tail_first, ab, conv_w, alog, dtb)


def _delta_seq_kernel(wq_ref, u0_ref, qk_ref, kdt_ref, cdec_ref, z_ref, s0_ref, gdnw_ref,
                      o_ref, sout_ref, s_sc, *, nbs, group):
    i = pl.program_id(0)

    @pl.when(i == 0)
    def _():
        s_sc[...] = s0_ref[...]

    gdnw = gdnw_ref[...]
    for b0 in range(0, nbs, group):
        chains = [(b, h) for b in range(b0, min(b0 + group, nbs)) for h in range(N_HEADS_B)]
        s_old = [s_sc[b, h] for b, h in chains]
        ws_qs = [_dot(wq_ref[b, h], s.astype(BF16)) for (b, h), s in zip(chains, s_old)]
        u_new = [(u0_ref[b, h] - x[:BLK]).astype(BF16) for (b, h), x in zip(chains, ws_qs)]
        for (b, h), s, x, u in zip(chains, s_old, ws_qs, u_new):
            o = x[BLK:] + _dot(qk_ref[b, h], u)
            s_sc[b, h] = cdec_ref[b, h:h + 1, :] * s + _dot(kdt_ref[b, h], u)
            zh = z_ref[b, :, h * DK:(h + 1) * DK]
            o_ref[b, :, h * DK:(h + 1) * DK] = _rms(o, gdnw) * (zh * _sigmoid(zh))

    @pl.when(i == pl.num_programs(0) - 1)
    def _():
        sout_ref[...] = s_sc[...]


def _delta_seq(wq, u0, qk, kdt, cdec, z, s0, gdnw, group):
    nbs, n_blk = wq.shape[:2]

    def blk(*tail_shape):
        return pl.BlockSpec((nbs, None) + tail_shape, lambda i: (0, i) + (0,) * len(tail_shape))

    state = pl.BlockSpec((nbs, N_HEADS_B, DK, DK), lambda i: (0, 0, 0, 0))
    kern = functools.partial(_delta_seq_kernel, nbs=nbs, group=group)
    return pl.pallas_call(
        kern,
        grid=(n_blk,),
        in_specs=[
            blk(N_HEADS_B, 2 * BLK, DK), blk(N_HEADS_B, BLK, DK), blk(N_HEADS_B, BLK, BLK),
            blk(N_HEADS_B, DK, BLK), blk(SUBLANES, BLK),
            pl.BlockSpec((nbs, BLK, V_B), lambda i: (0, i, 0)),
            state,
            pl.BlockSpec((1, DK), lambda i: (0, 0)),
        ],
        out_specs=[pl.BlockSpec((nbs, BLK, V_B), lambda i: (0, i, 0)), state],
        out_shape=[jax.ShapeDtypeStruct((nbs, n_blk * BLK, V_B), F32),
                   jax.ShapeDtypeStruct((nbs, N_HEADS_B, DK, DK), F32)],
        scratch_shapes=[pltpu.VMEM((nbs, N_HEADS_B, DK, DK), F32)],
        compiler_params=_ARB1,
        name="delta_seq",
    )(wq, u0, qk, kdt, cdec, z, s0, gdnw)


def _delta_sample_kernel(conv_ref, ab_ref, z_ref, tail0_ref, s0_ref, convw_ref, alog_ref, dtb_ref, gdnw_ref,
                         o_ref, sout_ref, *, tb, n_valid):
    def pad_rows(x):
        return jnp.concatenate([x, jnp.zeros((BLK - tb, x.shape[1]), F32)], axis=0)

    cb = _conv_silu(pad_rows(conv_ref[...]), tail0_ref[...], convw_ref)
    r = lax.broadcasted_iota(jnp.int32, (BLK, BLK), 0)
    c = lax.broadcasted_iota(jnp.int32, (BLK, BLK), 1)
    incl = r >= c
    strict = r > c
    row_ok = r < n_valid
    gb = jnp.where(row_ok, _decay_gates(pad_rows(ab_ref[...]), alog_ref, dtb_ref, c), 0.0)
    gc_rows = _prefix_sum_lanes(gb.T[0:SUBLANES], c[0:SUBLANES])
    gc_cols = jnp.concatenate([gc_rows, jnp.zeros((BLK - SUBLANES, BLK), F32)], axis=0).T
    levels = max(1, (n_valid - 1).bit_length())

    for h in range(N_HEADS_B):
        sl = slice(h * DK, (h + 1) * DK)
        qh = cb[:, h * DK:(h + 1) * DK]
        kh = cb[:, QK_B + h * DK:QK_B + (h + 1) * DK]
        vh = cb[:, 2 * QK_B + h * DK:2 * QK_B + (h + 1) * DK]
        qh = qh * lax.rsqrt(jnp.sum(qh * qh, axis=-1, keepdims=True) + EPS) * (DK ** -0.5)
        kh = kh * lax.rsqrt(jnp.sum(kh * kh, axis=-1, keepdims=True) + EPS)
        qh = jnp.where(row_ok, qh, 0.0)
        kh = jnp.where(row_ok, kh, 0.0)
        beta = gb[:, N_HEADS_B + h:N_HEADS_B + h + 1]
        gcc = gc_cols[:, h:h + 1]
        gcr = gc_rows[h:h + 1, :]
        gc_last = gc_rows[h:h + 1, BLK - 1:BLK]
        decay = jnp.where(incl, jnp.exp(jnp.where(incl, gcc - gcr, 0.0)), 0.0)
        eg = jnp.exp(gcc)
        kb = kh.astype(BF16)
        n_mat = jnp.where(strict, beta * decay * _dot_nt(kb, kb), 0.0)
        t_inv = _unit_lower_inverse(n_mat, r, c, levels)
        rhs = jnp.concatenate([beta * eg * kh, beta * vh], axis=1).astype(BF16)
        wu = _dot(t_inv.astype(BF16), rhs)
        qk = (_dot_nt(qh.astype(BF16), kb) * decay).astype(BF16)
        s_old = s0_ref[h]
        ws_qs = _dot(jnp.concatenate([wu[:, :DK], qh * eg], axis=0).astype(BF16), s_old.astype(BF16))
        u_new = (wu[:, DK:] - ws_qs[:BLK]).astype(BF16)
        o = ws_qs[BLK:] + _dot(qk, u_new)
        k_dec = (kh * jnp.exp(gc_last - gcc)).T.astype(BF16)
        sout_ref[h] = jnp.exp(gc_last) * s_old + _dot(k_dec, u_new)
        zh = pad_rows(z_ref[:, sl])
        og = _rms(o, gdnw_ref[...]) * (zh * _sigmoid(zh))
        o_ref[:, sl] = og[:tb]


def _delta_sample(conv_in, ab, z, tail0, s0, conv_w, alog, dtb, gdnw, n_valid):
    ns, tb, _ = conv_in.shape

    def seq(*shape):
        return pl.BlockSpec((None,) + shape, lambda b: (b,) + (0,) * len(shape))

    def const(shape):
        return pl.BlockSpec(shape, lambda b: (0,) * len(shape))

    kern = functools.partial(_delta_sample_kernel, tb=tb, n_valid=n_valid)
    return pl.pallas_call(
        kern,
        grid=(ns,),
        in_specs=[
            seq(tb, CONV_DIM), seq(tb, LANES), seq(tb, V_B), seq(SUBLANES, CONV_DIM), seq(N_HEADS_B, DK, DK),
            const((CONV_W, CONV_DIM)), const((1, LANES)), const((1, LANES)), const((1, DK)),
        ],
        out_specs=[seq(tb, V_B), seq(N_HEADS_B, DK, DK)],
        out_shape=[jax.ShapeDtypeStruct((ns, tb, V_B), F32), jax.ShapeDtypeStruct((ns, N_HEADS_B, DK, DK), F32)],
        compiler_params=_ARB1,
        name="delta_sample",
    )(conv_in, ab, z, tail0, s0, conv_w, alog, dtb, gdnw)


def _ffn_kernel(x_ref, oa_ref, ob_ref, g_ref, woa_ref, wob_ref, wout_ref, nffn_ref, wg_ref, wu_ref, wd_ref,
                nfin_ref, y_ref, acc_ref):
    ga = g_ref[:, :D_MODEL]
    gb = g_ref[:, D_MODEL:]
    merged = ga * _dot(oa_ref[...].astype(BF16), woa_ref[...]) + gb * _dot(ob_ref[...].astype(BF16), wob_ref[...])
    h = x_ref[...] + _dot(merged.astype(BF16), wout_ref[...])
    hn = _rms(h, nffn_ref[...]).astype(BF16)
    acc_ref[...] = h
    for c0 in range(0, D_FF, FF_CHUNK):
        gate = _dot(hn, wg_ref[:, c0:c0 + FF_CHUNK])
        up = _dot(hn, wu_ref[:, c0:c0 + FF_CHUNK])
        act = (gate * _sigmoid(gate) * up).astype(BF16)
        acc_ref[...] += _dot(act, wd_ref[c0:c0 + FF_CHUNK, :])
    y_ref[...] = _rms(acc_ref[...], nfin_ref[...])


def _ffn(x, oa, ob, gates, w_oa, w_ob, w_out, norm_ffn, w_gate, w_up, w_down, norm_final, tm):
    nb, t, _ = x.shape
    assert t % tm == 0

    def row(width):
        return pl.BlockSpec((None, tm, width), lambda b, i: (b, i, 0))

    def const(shape):
        return pl.BlockSpec(shape, lambda b, i: (0, 0), pipeline_mode=pl.Buffered(1))

    return pl.pallas_call(
        _ffn_kernel,
        grid=(nb, t // tm),
        in_specs=[
            row(D_MODEL), row(Q_A), row(V_B), row(2 * D_MODEL),
            const((Q_A, D_MODEL)), const((V_B, D_MODEL)), const((D_MODEL, D_MODEL)), const((1, D_MODEL)),
            const((D_MODEL, D_FF)), const((D_MODEL, D_FF)), const((D_FF, D_MODEL)), const((1, D_MODEL)),
        ],
        out_specs=row(D_MODEL),
        out_shape=jax.ShapeDtypeStruct((nb, t, D_MODEL), F32),
        scratch_shapes=[pltpu.VMEM((tm, D_MODEL), F32)],
        compiler_params=_ARB2,
        name="ffn",
    )(x, oa, ob, gates, w_oa, w_ob, w_out, norm_ffn, w_gate, w_up, w_down, norm_final)


def _rope_tables(pos):
    half = HEAD_DIM // 2
    inv = ROPE_THETA ** (-jnp.arange(half, dtype=F32) / half)
    ang = pos.astype(F32)[:, None] * inv[None, :]
    cos, sin = jnp.cos(ang), jnp.sin(ang)
    reps = LANES // HEAD_DIM
    return (jnp.tile(jnp.concatenate([cos, cos], axis=1), (1, reps)),
            jnp.tile(jnp.concatenate([-sin, sin], axis=1), (1, reps)))


def _pad_seq_rows(x, rows, front=0):
    return jnp.pad(x, ((0, 0), (front, rows - front - x.shape[1]), (0, 0)))


def kernel(x_prompt, x_sample, cache_swa_k, cache_swa_v, state_delta, state_conv, meta_tokens, norm_mix, w_in,
           conv_w, a_log, dt_bias, sinks, gdn_norm, w_oa, w_ob, w_out, norm_ffn, w_gate, w_up, w_down,
           norm_final):
    assert w_in.shape[0] == 1, "single trunk layer"
    nb, seq, _ = x_prompt.shape
    ns, n_new, _ = x_sample.shape
    assert seq % BLK == 0 and n_new <= SUBLANES

    w = w_in[0]
    c_ab = Q_A + 2 * KV_A + CONV_DIM + V_B
    w_packed = jnp.concatenate(
        [w[:, :c_ab], jnp.pad(w[:, c_ab:c_ab + 2 * N_HEADS_B], ((0, 0), (0, LANES - 2 * N_HEADS_B))),
         w[:, c_ab + 2 * N_HEADS_B:]], axis=1).astype(BF16)
    nmix = norm_mix[0][None, :]
    lane_pad = (0, LANES - N_HEADS_B)
    alog = jnp.pad(a_log[0], lane_pad)[None, :]
    dtb = jnp.pad(dt_bias[0], lane_pad)[None, :]
    gdnw = gdn_norm[0][None, :]
    cw = conv_w[0]
    ffn_w = (w_oa[0].astype(BF16), w_ob[0].astype(BF16), w_out[0].astype(BF16), norm_ffn[0][None, :],
             w_gate[0].astype(BF16), w_up[0].astype(BF16), w_down[0].astype(BF16), norm_final[None, :])
    sk = sinks[0]

    x_meta = jnp.concatenate([jnp.zeros((PAD_ROWS, D_MODEL), F32), meta_tokens.astype(F32)], axis=0)[None]
    cos_m, sin_m = _rope_tables(jnp.arange(BLK, dtype=jnp.int32) - PAD_ROWS)
    _, k_m, v_m, conv_m, z_m, ab_m, _ = _proj(x_meta, nmix, w_packed, cos_m, sin_m, tm=BLK)
    zero_tail = jnp.zeros((SUBLANES, CONV_DIM), F32)
    prep_m = _delta_prep(conv_m, ab_m, zero_tail, cw, alog, dtb, g_blocks=1)
    _, s_meta = _delta_seq(*prep_m, z_m, jnp.zeros((1, N_HEADS_B, DK, DK), F32), gdnw, group=1)

    cos_p, sin_p = _rope_tables(N_META + jnp.arange(seq, dtype=jnp.int32))
    q_p, k_p, v_p, conv_p, z_p, ab_p, gate_p = _proj(x_prompt, nmix, w_packed, cos_p, sin_p, tm=4 * BLK)
    oa_p = _attn_prompt(sk, q_p, k_p, v_p, k_m[0], v_m[0])
    prep_p = _delta_prep(conv_p, ab_p, conv_m[0, BLK - SUBLANES:], cw, alog, dtb, g_blocks=4)
    ob_p, s_p = _delta_seq(*prep_p, z_p, jnp.broadcast_to(s_meta, (nb, N_HEADS_B, DK, DK)), gdnw, group=4)
    y_prompt = _ffn(x_prompt, oa_p, ob_p, gate_p, *ffn_w, tm=2 * BLK)

    rows_s = ns * n_new
    pos_s = PAST_LEN + (jnp.arange(rows_s, dtype=jnp.int32) % n_new)
    cos_s, sin_s = _rope_tables(pos_s)
    outs = _proj(x_sample.reshape(1, rows_s, D_MODEL), nmix, w_packed, cos_s, sin_s, tm=rows_s // 2)
    q_s, k_s, v_s, conv_s, z_s, ab_s, gate_s = [o.reshape(ns, n_new, o.shape[-1]) for o in outs]
    pad8 = functools.partial(_pad_seq_rows, rows=SUBLANES)
    oa_s, k_cache, v_cache = _attn_sample(
        sk, pad8(q_s), pad8(k_s), pad8(v_s), cache_swa_k[0].reshape(ns, BLK, KV_A),
        cache_swa_v[0].reshape(ns, BLK, KV_A), n_new, sb=8)
    tail_s = _pad_seq_rows(state_conv[0], SUBLANES, front=SUBLANES - (CONV_W - 1))
    ob_s, s_s = _delta_sample(pad8(conv_s), pad8(ab_s), pad8(z_s), tail_s, state_delta[0], cw, alog, dtb, gdnw,
                              n_valid=n_new)
    y_sample = _ffn(x_sample.reshape(1, rows_s, D_MODEL), oa_s[:, :n_new].reshape(1, rows_s, Q_A),
                    ob_s[:, :n_new].reshape(1, rows_s, V_B), gate_s.reshape(1, rows_s, 2 * D_MODEL),
                    *ffn_w, tm=rows_s // 2).reshape(ns, n_new, D_MODEL)

    kv_shape = (1, -1, BLK, N_KV_HEADS, HEAD_DIM)
    conv_new_s = jnp.concatenate([state_conv[0], conv_s], axis=1)[:, -(CONV_W - 1):]
    return (y_prompt, y_sample,
            k_p[:, seq - BLK:].reshape(kv_shape), v_p[:, seq - BLK:].reshape(kv_shape),
            s_p[None], conv_p[:, seq - (CONV_W - 1):][None],
            k_cache.reshape(kv_shape), v_cache.reshape(kv_shape), s_s[None], conv_new_s[None])
```

```python
import functools

import jax
import jax.numpy as jnp
from jax import lax
from jax.experimental import pallas as pl
from jax.experimental.pallas import tpu as pltpu

F32 = jnp.float32
BF16 = jnp.bfloat16

D_MODEL = 1024
N_META = 16
EPS = 1e-6
HEAD_DIM = 64
N_HEADS_A = 8
N_KV_HEADS = 2
BLK = 128
ROPE_THETA = 10000.0
PAST_LEN = 16384
DK = 128
N_HEADS_B = 4
CONV_W = 4
D_FF = 2816
Q_A = N_HEADS_A * HEAD_DIM
KV_A = N_KV_HEADS * HEAD_DIM
QK_B = N_HEADS_B * DK
V_B = N_HEADS_B * DK
CONV_DIM = 2 * QK_B + V_B
LANES = 128
SUBLANES = 8
PAD_ROWS = BLK - N_META
FF_CHUNK = 256
VMEM_LIMIT = 56 * 1024 * 1024

C_QKV = (0, 768)
C_CONV = (768, 2304)
C_ZAB = (2304, 2944)
C_GATE = (2944, 4992)
W_COLS = 4992

_ARB2 = pltpu.CompilerParams(dimension_semantics=("arbitrary", "arbitrary"), vmem_limit_bytes=VMEM_LIMIT)
_ARB1 = pltpu.CompilerParams(dimension_semantics=("arbitrary",), vmem_limit_bytes=VMEM_LIMIT)


def _dot(a, b):
    return jnp.dot(a, b, preferred_element_type=F32)


def _dot_nt(a, b):
    return lax.dot_general(a, b, (((1,), (1,)), ((), ())), preferred_element_type=F32)


def _bmm(a, b):
    return lax.dot_general(a, b, (((2,), (1,)), ((0,), (0,))), preferred_element_type=F32)


def _sigmoid(x):
    return 1.0 / (1.0 + jnp.exp(-x))


def _rms(x, w):
    return x * lax.rsqrt(jnp.mean(x * x, axis=-1, keepdims=True) + EPS) * w


def _proj_kernel(x_ref, nw_ref, w_ref, cos_ref, sin_ref,
                 q_ref, k_ref, v_ref, conv_ref, z_ref, ab_ref, gate_ref):
    xn = _rms(x_ref[...], nw_ref[...]).astype(BF16)
    cos = cos_ref[...]
    sin = sin_ref[...]
    lane = lax.broadcasted_iota(jnp.int32, cos.shape, 1)
    first_half = (lane & (HEAD_DIM - 1)) < (HEAD_DIM // 2)

    def rope(t):
        rot = jnp.where(first_half, pltpu.roll(t, LANES - HEAD_DIM // 2, 1), pltpu.roll(t, HEAD_DIM // 2, 1))
        return t * cos + rot * sin

    qkv = _dot(xn, w_ref[:, C_QKV[0]:C_QKV[1]])
    for j in range(Q_A // LANES):
        q_ref[:, j * LANES:(j + 1) * LANES] = rope(qkv[:, j * LANES:(j + 1) * LANES]) * (HEAD_DIM ** -0.5)
    k_ref[...] = rope(qkv[:, Q_A:Q_A + KV_A])
    v_ref[...] = qkv[:, Q_A + KV_A:Q_A + 2 * KV_A]
    conv_ref[...] = _dot(xn, w_ref[:, C_CONV[0]:C_CONV[1]])
    zab = _dot(xn, w_ref[:, C_ZAB[0]:C_ZAB[1]])
    z_ref[...] = zab[:, :V_B]
    ab_ref[...] = zab[:, V_B:]
    gate_ref[...] = _sigmoid(_dot(xn, w_ref[:, C_GATE[0]:C_GATE[1]]))


def _proj(x, norm_w, w_packed, cos_t, sin_t, tm):
    nb, t, _ = x.shape
    assert t % tm == 0

    def row(width):
        return pl.BlockSpec((None, tm, width), lambda b, i: (b, i, 0))

    widths = (Q_A, KV_A, KV_A, CONV_DIM, V_B, LANES, 2 * D_MODEL)
    return pl.pallas_call(
        _proj_kernel,
        grid=(nb, t // tm),
        in_specs=[
            row(D_MODEL),
            pl.BlockSpec((1, D_MODEL), lambda b, i: (0, 0)),
            pl.BlockSpec((D_MODEL, W_COLS), lambda b, i: (0, 0), pipeline_mode=pl.Buffered(1)),
            pl.BlockSpec((tm, LANES), lambda b, i: (i, 0)),
            pl.BlockSpec((tm, LANES), lambda b, i: (i, 0)),
        ],
        out_specs=[row(w) for w in widths],
        out_shape=[jax.ShapeDtypeStruct((nb, t, w), F32) for w in widths],
        compiler_params=_ARB2,
        name="proj",
    )(x, norm_w, w_packed, cos_t, sin_t)


def _attend(sinks_ref, items, tq):
    lane = lax.broadcasted_iota(jnp.int32, (BLK, LANES), 1)
    lo_k = lane < HEAD_DIM
    lo_q = lane[:tq] < HEAD_DIM
    rows = lax.broadcasted_iota(jnp.int32, (2 * tq, BLK), 0)
    qi = jnp.where(rows < tq, rows, rows - tq)
    kj = lax.broadcasted_iota(jnp.int32, (2 * tq, BLK), 1)
    mask_c = kj <= qi
    row1 = lax.broadcasted_iota(jnp.int32, (2 * tq, 1), 0)
    ones = jnp.ones((BLK, LANES), BF16)

    def dup(x):
        r = pltpu.roll(x, HEAD_DIM, 1)
        return jnp.where(lo_k, x, r).astype(BF16), jnp.where(lo_k, r, x).astype(BF16)

    chains = []
    for it, (q_of_group, kp, vp, kc, vc, mask_p) in enumerate(items):
        kdp, vdp, kdc, vdc = dup(kp), dup(vp), dup(kc), dup(vc)
        for j in range(Q_A // LANES):
            kv = j // 2
            q2 = q_of_group(j)
            qs = jnp.concatenate([jnp.where(lo_q, q2, 0.0), jnp.where(lo_q, 0.0, q2)], axis=0).astype(BF16)
            chains.append((it, j, qs, kdp[kv], vdp[kv], kdc[kv], vdc[kv], mask_p))
    scores = [(jnp.where(ch[7], _dot_nt(ch[2], ch[3]), -jnp.inf), jnp.where(mask_c, _dot_nt(ch[2], ch[5]), -jnp.inf))
              for ch in chains]
    probs = []
    for ch, (sp, sc) in zip(chains, scores):
        sk = jnp.where(row1 < tq, sinks_ref[2 * ch[1]], sinks_ref[2 * ch[1] + 1])
        m = jnp.maximum(jnp.max(jnp.maximum(sp, sc), axis=-1, keepdims=True), sk)
        probs.append((jnp.exp(sp - m), jnp.exp(sc - m), jnp.exp(sk - m)))
    outs = [[None] * (Q_A // LANES) for _ in items]
    for ch, (pp, pc, ps) in zip(chains, probs):
        num = _dot(pp.astype(BF16), ch[4]) + _dot(pc.astype(BF16), ch[6])
        den = _dot((pp + pc).astype(BF16), ones) + ps
        o = num / den
        outs[ch[0]][ch[1]] = jnp.where(lo_q, o[:tq], o[tq:])
    return outs


def _attn_prompt_kernel(sinks_ref, q_ref, kp_ref, vp_ref, kc_ref, vc_ref, km_ref, vm_ref, o_ref, *, g_blocks):
    first = pl.program_id(1) == 0
    pv = jnp.where(first, PAD_ROWS, 0)
    rows = lax.broadcasted_iota(jnp.int32, (2 * BLK, BLK), 0)
    qi = jnp.where(rows < BLK, rows, rows - BLK)
    kj = lax.broadcasted_iota(jnp.int32, (2 * BLK, BLK), 1)
    mask_p = kj >= qi
    items = []
    for jb in range(g_blocks):
        rs = slice(jb * BLK, (jb + 1) * BLK)
        if jb == 0:
            kp = jnp.where(first, km_ref[...], kp_ref[...])
            vp = jnp.where(first, vm_ref[...], vp_ref[...])
            mp = mask_p & (kj >= pv)
        else:
            ps = slice((jb - 1) * BLK, jb * BLK)
            kp, vp, mp = kc_ref[ps, :], vc_ref[ps, :], mask_p
        items.append((lambda j, rs=rs: q_ref[rs, j * LANES:(j + 1) * LANES], kp, vp, kc_ref[rs, :], vc_ref[rs, :], mp))
    outs = _attend(sinks_ref, items, BLK)
    for jb in range(g_blocks):
        for j in range(Q_A // LANES):
            o_ref[jb * BLK:(jb + 1) * BLK, j * LANES:(j + 1) * LANES] = outs[jb][j]


def _attn_prompt(sinks, q, k, v, k_meta, v_meta, g_blocks):
    nb, t, _ = q.shape
    rows = g_blocks * BLK
    assert t % rows == 0

    def cur(b, i):
        return (b, i, 0)

    def prev(b, i):
        return (b, jnp.maximum(i * g_blocks - 1, 0), 0)

    meta_spec = pl.BlockSpec((BLK, KV_A), lambda b, i: (0, 0))
    return pl.pallas_call(
        functools.partial(_attn_prompt_kernel, g_blocks=g_blocks),
        grid=(nb, t // rows),
        in_specs=[
            pl.BlockSpec(memory_space=pltpu.SMEM),
            pl.BlockSpec((None, rows, Q_A), cur),
            pl.BlockSpec((None, BLK, KV_A), prev),
            pl.BlockSpec((None, BLK, KV_A), prev),
            pl.BlockSpec((None, rows, KV_A), cur),
            pl.BlockSpec((None, rows, KV_A), cur),
            meta_spec, meta_spec,
        ],
        out_specs=pl.BlockSpec((None, rows, Q_A), cur),
        out_shape=jax.ShapeDtypeStruct((nb, t, Q_A), F32),
        compiler_params=_ARB2,
        name="attn_prompt",
    )(sinks, q, k, v, k, v, k_meta, v_meta)


def _attn_sample_kernel(sinks_ref, q_ref, kp_ref, vp_ref, kc_ref, vc_ref, o_ref, ko_ref, vo_ref, *, sb, tq, n_new):
    rows = lax.broadcasted_iota(jnp.int32, (2 * tq, BLK), 0)
    qi = jnp.where(rows < tq, rows, rows - tq)
    kj = lax.broadcasted_iota(jnp.int32, (2 * tq, BLK), 1)
    mask_p = kj >= qi

    def pad_rows(x):
        return jnp.concatenate([x, jnp.zeros((BLK - tq, LANES), F32)], axis=0)

    items = [(lambda j, s=s: q_ref[s, :, j * LANES:(j + 1) * LANES], kp_ref[s], vp_ref[s],
              pad_rows(kc_ref[s]), pad_rows(vc_ref[s]), mask_p) for s in range(sb)]
    outs = _attend(sinks_ref, items, tq)
    for s in range(sb):
        for j in range(Q_A // LANES):
            o_ref[s, :, j * LANES:(j + 1) * LANES] = outs[s][j]
        ko_ref[s, 0:BLK - n_new, :] = kp_ref[s, n_new:BLK, :]
        ko_ref[s, BLK - n_new:BLK, :] = kc_ref[s, 0:n_new, :]
        vo_ref[s, 0:BLK - n_new, :] = vp_ref[s, n_new:BLK, :]
        vo_ref[s, BLK - n_new:BLK, :] = vc_ref[s, 0:n_new, :]


def _attn_sample(sinks, q, k_new, v_new, k_past, v_past, n_new, sb):
    ns, tq, _ = q.shape
    assert ns % sb == 0

    def idx(g):
        return (g, 0, 0)

    cache = jax.ShapeDtypeStruct((ns, BLK, KV_A), F32)
    return pl.pallas_call(
        functools.partial(_attn_sample_kernel, sb=sb, tq=tq, n_new=n_new),
        grid=(ns // sb,),
        in_specs=[
            pl.BlockSpec(memory_space=pltpu.SMEM),
            pl.BlockSpec((sb, tq, Q_A), idx),
            pl.BlockSpec((sb, BLK, KV_A), idx),
            pl.BlockSpec((sb, BLK, KV_A), idx),
            pl.BlockSpec((sb, tq, KV_A), idx),
            pl.BlockSpec((sb, tq, KV_A), idx),
        ],
        out_specs=[pl.BlockSpec((sb, tq, Q_A), idx), pl.BlockSpec((sb, BLK, KV_A), idx),
                   pl.BlockSpec((sb, BLK, KV_A), idx)],
        out_shape=[jax.ShapeDtypeStruct((ns, tq, Q_A), F32), cache, cache],
        compiler_params=_ARB1,
        name="attn_sample",
    )(sinks, q, k_past, v_past, k_new, v_new)


def _unit_lower_inverse(n_mat, r, c, levels):
    x = (r == c).astype(F32)[None] - jnp.where((((r ^ c) == 1) & (r > c))[None], n_mat, 0.0)
    for lg in range(1, levels):
        same_2g = (r >> (lg + 1)) == (c >> (lg + 1))
        joins = same_2g & (((r >> lg) & 1) == 1) & (((c >> lg) & 1) == 0)
        e = jnp.where(joins[None], n_mat, 0.0).astype(BF16)
        xb = x.astype(BF16)
        x = x - _bmm(_bmm(xb, e).astype(BF16), xb)
    return x


def _conv_silu(ext_ref, n, convw_ref):
    y = ext_ref[5:5 + n, :] * convw_ref[0:1, :]
    y = y + ext_ref[6:6 + n, :] * convw_ref[1:2, :]
    y = y + ext_ref[7:7 + n, :] * convw_ref[2:3, :]
    y = y + ext_ref[8:8 + n, :] * convw_ref[3:4, :]
    return y * _sigmoid(y)


def _decay_gates(ab, alog_ref, dtb_ref, lane):
    xg = ab + dtb_ref[...]
    softplus = jnp.maximum(xg, 0.0) + jnp.log1p(jnp.exp(-jnp.abs(xg)))
    return jnp.where(lane < N_HEADS_B, -jnp.exp(alog_ref[...]) * softplus, _sigmoid(ab))


def _segment_sums(x, pos, seg):
    pre = x
    sh = 1
    while sh < seg:
        pre = pre + jnp.where(pos >= sh, pltpu.roll(pre, sh, 1), 0.0)
        sh *= 2
    if seg == x.shape[-1]:
        return pre, jnp.broadcast_to(pre[:, seg - 1:seg], x.shape)
    suf = x
    sh = 1
    while sh < seg:
        suf = suf + jnp.where(pos < seg - sh, pltpu.roll(suf, x.shape[-1] - sh, 1), 0.0)
        sh *= 2
    return pre, pre + suf - x


def _l2norm_rows(x):
    ss = jnp.broadcast_to(jnp.sum(x * x, axis=-1, keepdims=True), x.shape)
    return x * lax.rsqrt(ss + EPS)


def _delta_prep_kernel(conv_ref, tailp_ref, tailf_ref, ab_ref, convw_ref, alog_ref, dtb_ref,
                       wq_ref, u0_ref, qk_ref, kdt_ref, cdec_ref, ext_sc, n_sc, rhs_sc, *, g_blocks, seg, valid):
    rows = g_blocks * BLK
    ext_sc[0:SUBLANES, :] = jnp.where(pl.program_id(1) == 0, tailf_ref[...], tailp_ref[...])
    ext_sc[SUBLANES:SUBLANES + rows, :] = conv_ref[...]
    cb = _conv_silu(ext_sc, rows, convw_ref)
    lane = lax.broadcasted_iota(jnp.int32, (rows, LANES), 1)
    gb = _decay_gates(ab_ref[...], alog_ref, dtb_ref, lane)

    r = lax.broadcasted_iota(jnp.int32, (BLK, BLK), 0)
    c = lax.broadcasted_iota(jnp.int32, (BLK, BLK), 1)
    incl = r >= c
    strict = r > c
    row_ok = None
    if seg < BLK:
        same_seg = (r & -seg) == (c & -seg)
        incl = incl & same_seg
        strict = strict & same_seg
    if valid is not None:
        pos_r = r & (seg - 1)
        row_ok = (pos_r >= valid[0]) & (pos_r < valid[1])
    pos_lane = c[0:SUBLANES] & (seg - 1)
    for j in range(g_blocks):
        rs = slice(j * BLK, (j + 1) * BLK)
        gbj = gb[rs]
        if row_ok is not None:
            gbj = jnp.where(row_ok, gbj, 0.0)
        g_rows = gbj.T[0:SUBLANES]
        gc_rows, tot_rows = _segment_sums(g_rows, pos_lane, seg)
        cols = jnp.concatenate([gc_rows, tot_rows, jnp.zeros((BLK - 2 * SUBLANES, BLK), F32)], axis=0).T
        cdec_ref[j] = jnp.exp(tot_rows)
        for h in range(N_HEADS_B):
            qh = _l2norm_rows(cb[rs, h * DK:(h + 1) * DK]) * (DK ** -0.5)
            kh = _l2norm_rows(cb[rs, QK_B + h * DK:QK_B + (h + 1) * DK])
            vh = cb[rs, 2 * QK_B + h * DK:2 * QK_B + (h + 1) * DK]
            if row_ok is not None:
                qh = jnp.where(row_ok, qh, 0.0)
                kh = jnp.where(row_ok, kh, 0.0)
            beta = gbj[:, N_HEADS_B + h:N_HEADS_B + h + 1]
            gcc = cols[:, h:h + 1]
            gct = cols[:, SUBLANES + h:SUBLANES + h + 1]
            gcr = gc_rows[h:h + 1, :]
            decay = jnp.where(incl, jnp.exp(jnp.where(incl, gcc - gcr, 0.0)), 0.0)
            eg = jnp.exp(gcc)
            kb = kh.astype(BF16)
            n_sc[j * N_HEADS_B + h] = jnp.where(strict, beta * decay * _dot_nt(kb, kb), 0.0)
            rhs_sc[j * N_HEADS_B + h] = jnp.concatenate([beta * eg * kh, beta * vh], axis=1).astype(BF16)
            qk_ref[j, h] = (_dot_nt(qh.astype(BF16), kb) * decay).astype(BF16)
            wq_ref[j, h, BLK:2 * BLK, :] = (qh * eg).astype(BF16)
            kdt_ref[j, h] = (kh * jnp.exp(gct - gcc)).T.astype(BF16)

    t_inv = _unit_lower_inverse(n_sc[...], r, c, levels=seg.bit_length() - 1)
    wu = _bmm(t_inv.astype(BF16), rhs_sc[...])
    for j in range(g_blocks):
        for h in range(N_HEADS_B):
            wq_ref[j, h, 0:BLK, :] = wu[j * N_HEADS_B + h, :, :DK].astype(BF16)
            u0_ref[j, h] = wu[j * N_HEADS_B + h, :, DK:]


def _delta_prep(conv_in, ab, tail_first, conv_w, alog, dtb, g_blocks, seg=BLK, valid=None):
    nb, t, _ = conv_in.shape
    n_blk = t // BLK
    assert n_blk % g_blocks == 0
    rows = g_blocks * BLK
    chains = g_blocks * N_HEADS_B

    def const(shape):
        return pl.BlockSpec(shape, lambda b, j: (0,) * len(shape))

    def out(*tail_shape):
        return pl.BlockSpec((None, g_blocks) + tail_shape, lambda b, j: (b, j) + (0,) * len(tail_shape))

    kern = functools.partial(_delta_prep_kernel, g_blocks=g_blocks, seg=seg, valid=valid)
    return pl.pallas_call(
        kern,
        grid=(nb, n_blk // g_blocks),
        in_specs=[
            pl.BlockSpec((None, rows, CONV_DIM), lambda b, j: (b, j, 0)),
            pl.BlockSpec((None, SUBLANES, CONV_DIM),
                         lambda b, j: (b, jnp.maximum(j * (rows // SUBLANES) - 1, 0), 0)),
            const((SUBLANES, CONV_DIM)),
            pl.BlockSpec((None, rows, LANES), lambda b, j: (b, j, 0)),
            const((CONV_W, CONV_DIM)), const((1, LANES)), const((1, LANES)),
        ],
        out_specs=[out(N_HEADS_B, 2 * BLK, DK), out(N_HEADS_B, BLK, DK), out(N_HEADS_B, BLK, BLK),
                   out(N_HEADS_B, DK, BLK), out(SUBLANES, BLK)],
        out_shape=[
            jax.ShapeDtypeStruct((nb, n_blk, N_HEADS_B, 2 * BLK, DK), BF16),
            jax.ShapeDtypeStruct((nb, n_blk, N_HEADS_B, BLK, DK), F32),
            jax.ShapeDtypeStruct((nb, n_blk, N_HEADS_B, BLK, BLK), BF16),
            jax.ShapeDtypeStruct((nb, n_blk, N_HEADS_B, DK, BLK), BF16),
            jax.ShapeDtypeStruct((nb, n_blk, SUBLANES, BLK), F32),
        ],
        scratch_shapes=[pltpu.VMEM((SUBLANES + rows, CONV_DIM), F32), pltpu.VMEM((chains, BLK, BLK), F32),
                        pltpu.VMEM((chains, BLK, 2 * DK), BF16)],
        compiler_params=_ARB2,
        name="delta_prep",
    )(conv_in, conv_in, tail_first, ab, conv_w, alog, dtb)


def _gated_norm(o, z, gdnw):
    return _rms(o, gdnw) * (z * _sigmoid(z))


def _delta_seq_kernel(wq_ref, u0_ref, qk_ref, kdt_ref, cdec_ref, z_ref, s0_ref, gdnw_ref,
                      o_ref, sout_ref, s_sc, *, nbs, group):
    i = pl.program_id(0)

    @pl.when(i == 0)
    def _():
        s_sc[...] = s0_ref[...]

    gdnw = gdnw_ref[...]
    for b0 in range(0, nbs, group):
        chains = [(b, h) for b in range(b0, min(b0 + group, nbs)) for h in range(N_HEADS_B)]
        s_old = [s_sc[b, h] for b, h in chains]
        ws_qs = [_dot(wq_ref[b, h], s.astype(BF16)) for (b, h), s in zip(chains, s_old)]
        u_new = [(u0_ref[b, h] - x[:BLK]).astype(BF16) for (b, h), x in zip(chains, ws_qs)]
        for (b, h), s, x, u in zip(chains, s_old, ws_qs, u_new):
            o = x[BLK:] + _dot(qk_ref[b, h], u)
            s_sc[b, h] = cdec_ref[b, h:h + 1, :] * s + _dot(kdt_ref[b, h], u)
            o_ref[b, :, h * DK:(h + 1) * DK] = _gated_norm(o, z_ref[b, :, h * DK:(h + 1) * DK], gdnw)

    @pl.when(i == pl.num_programs(0) - 1)
    def _():
        sout_ref[...] = s_sc[...]


def _delta_seq(wq, u0, qk, kdt, cdec, z, s0, gdnw, group):
    nbs, n_blk = wq.shape[:2]

    def blk(*tail_shape):
        return pl.BlockSpec((nbs, None) + tail_shape, lambda i: (0, i) + (0,) * len(tail_shape))

    state = pl.BlockSpec((nbs, N_HEADS_B, DK, DK), lambda i: (0, 0, 0, 0))
    kern = functools.partial(_delta_seq_kernel, nbs=nbs, group=group)
    return pl.pallas_call(
        kern,
        grid=(n_blk,),
        in_specs=[
            blk(N_HEADS_B, 2 * BLK, DK), blk(N_HEADS_B, BLK, DK), blk(N_HEADS_B, BLK, BLK),
            blk(N_HEADS_B, DK, BLK), blk(SUBLANES, BLK),
            pl.BlockSpec((nbs, BLK, V_B), lambda i: (0, i, 0)),
            state,
            pl.BlockSpec((1, DK), lambda i: (0, 0)),
        ],
        out_specs=[pl.BlockSpec((nbs, BLK, V_B), lambda i: (0, i, 0)), state],
        out_shape=[jax.ShapeDtypeStruct((nbs, n_blk * BLK, V_B), F32),
                   jax.ShapeDtypeStruct((nbs, N_HEADS_B, DK, DK), F32)],
        scratch_shapes=[pltpu.VMEM((nbs, N_HEADS_B, DK, DK), F32)],
        compiler_params=_ARB1,
        name="delta_seq",
    )(wq, u0, qk, kdt, cdec, z, s0, gdnw)


def _delta_seq_packed_kernel(wq_ref, u0_ref, qk_ref, kdt_ref, cdec_ref, z_ref, s0_ref, gdnw_ref,
                             o_ref, sout_ref, *, seg):
    n_seq = BLK // seg
    tile = 2 * SUBLANES
    r = lax.broadcasted_iota(jnp.int32, (BLK, DK), 0)
    gdnw = gdnw_ref[...]
    for h in range(N_HEADS_B):
        ws_parts, qs_parts = [], []
        for s in range(n_seq):
            t0 = (s * seg) // tile * tile
            off = s * seg - t0
            lhs = jnp.concatenate([wq_ref[h, t0:t0 + tile, :], wq_ref[h, BLK + t0:BLK + t0 + tile, :]], axis=0)
            x = _dot(lhs, s0_ref[s, h].astype(BF16))
            ws_parts.append(x[off:off + seg])
            qs_parts.append(x[tile + off:tile + off + seg])
        u_new = u0_ref[h] - jnp.concatenate(ws_parts, axis=0)
        o = jnp.concatenate(qs_parts, axis=0) + _dot(qk_ref[h], u_new.astype(BF16))
        o_ref[:, h * DK:(h + 1) * DK] = _gated_norm(o, z_ref[:, h * DK:(h + 1) * DK], gdnw)
        kdt = kdt_ref[h]
        for s in range(n_seq):
            own = (r >= s * seg) & (r < (s + 1) * seg)
            u_own = jnp.where(own, u_new, 0.0).astype(BF16)
            sout_ref[s, h] = cdec_ref[h:h + 1, s * seg:s * seg + 1] * s0_ref[s, h] + _dot(kdt, u_own)


def _delta_seq_packed(wq, u0, qk, kdt, cdec, z, s0, gdnw, seg):
    n_blk = wq.shape[1]
    n_seq = BLK // seg
    assert wq.shape[0] == 1 and s0.shape[0] == n_blk * n_seq

    def blk(*tail_shape):
        return pl.BlockSpec((None, None) + tail_shape, lambda g: (0, g) + (0,) * len(tail_shape))

    state = pl.BlockSpec((n_seq, N_HEADS_B, DK, DK), lambda g: (g, 0, 0, 0))
    rows = pl.BlockSpec((None, BLK, V_B), lambda g: (0, g, 0))
    return pl.pallas_call(
        functools.partial(_delta_seq_packed_kernel, seg=seg),
        grid=(n_blk,),
        in_specs=[
            blk(N_HEADS_B, 2 * BLK, DK), blk(N_HEADS_B, BLK, DK), blk(N_HEADS_B, BLK, BLK),
            blk(N_HEADS_B, DK, BLK), blk(SUBLANES, BLK), rows, state,
            pl.BlockSpec((1, DK), lambda g: (0, 0)),
        ],
        out_specs=[rows, state],
        out_shape=[jax.ShapeDtypeStruct((1, n_blk * BLK, V_B), F32),
                   jax.ShapeDtypeStruct(s0.shape, F32)],
        compiler_params=_ARB1,
        name="delta_seq_packed",
    )(wq, u0, qk, kdt, cdec, z, s0, gdnw)


def _ffn_kernel(x_ref, oa_ref, ob_ref, g_ref, woa_ref, wob_ref, wout_ref, nffn_ref, wg_ref, wu_ref, wd_ref,
                nfin_ref, y_ref, acc_ref):
    ga = g_ref[:, :D_MODEL]
    gb = g_ref[:, D_MODEL:]
    merged = ga * _dot(oa_ref[...].astype(BF16), woa_ref[...]) + gb * _dot(ob_ref[...].astype(BF16), wob_ref[...])
    h = x_ref[...] + _dot(merged.astype(BF16), wout_ref[...])
    hn = _rms(h, nffn_ref[...]).astype(BF16)
    acc_ref[...] = h
    for c0 in range(0, D_FF, FF_CHUNK):
        gate = _dot(hn, wg_ref[:, c0:c0 + FF_CHUNK])
        up = _dot(hn, wu_ref[:, c0:c0 + FF_CHUNK])
        act = (gate * _sigmoid(gate) * up).astype(BF16)
        acc_ref[...] += _dot(act, wd_ref[c0:c0 + FF_CHUNK, :])
    y_ref[...] = _rms(acc_ref[...], nfin_ref[...])


def _ffn(x, oa, ob, gates, w_oa, w_ob, w_out, norm_ffn, w_gate, w_up, w_down, norm_final, tm):
    nb, t, _ = x.shape
    assert t % tm == 0

    def row(width):
        return pl.BlockSpec((None, tm, width), lambda b, i: (b, i, 0))

    def const(shape):
        return pl.BlockSpec(shape, lambda b, i: (0, 0), pipeline_mode=pl.Buffered(1))

    return pl.pallas_call(
        _ffn_kernel,
        grid=(nb, t // tm),
        in_specs=[
            row(D_MODEL), row(Q_A), row(V_B), row(2 * D_MODEL),
            const((Q_A, D_MODEL)), const((V_B, D_MODEL)), const((D_MODEL, D_MODEL)), const((1, D_MODEL)),
            const((D_MODEL, D_FF)), const((D_MODEL, D_FF)), const((D_FF, D_MODEL)), const((1, D_MODEL)),
        ],
        out_specs=row(D_MODEL),
        out_shape=jax.ShapeDtypeStruct((nb, t, D_MODEL), F32),
        scratch_shapes=[pltpu.VMEM((tm, D_MODEL), F32)],
        compiler_params=_ARB2,
        name="ffn",
    )(x, oa, ob, gates, w_oa, w_ob, w_out, norm_ffn, w_gate, w_up, w_down, norm_final)


def _rope_tables(pos):
    half = HEAD_DIM // 2
    inv = ROPE_THETA ** (-jnp.arange(half, dtype=F32) / half)
    ang = pos.astype(F32)[:, None] * inv[None, :]
    cos, sin = jnp.cos(ang), jnp.sin(ang)
    reps = LANES // HEAD_DIM
    return (jnp.tile(jnp.concatenate([cos, cos], axis=1), (1, reps)),
            jnp.tile(jnp.concatenate([-sin, sin], axis=1), (1, reps)))


def _pad_seq_rows(x, rows, front=0):
    return jnp.pad(x, ((0, 0), (front, rows - front - x.shape[1]), (0, 0)))


def kernel(x_prompt, x_sample, cache_swa_k, cache_swa_v, state_delta, state_conv, meta_tokens, norm_mix, w_in,
           conv_w, a_log, dt_bias, sinks, gdn_norm, w_oa, w_ob, w_out, norm_ffn, w_gate, w_up, w_down,
           norm_final):
    assert w_in.shape[0] == 1, "single trunk layer"
    nb, seq, _ = x_prompt.shape
    ns, n_new, _ = x_sample.shape
    assert seq % BLK == 0 and n_new <= SUBLANES

    w = w_in[0]
    c_ab = Q_A + 2 * KV_A + CONV_DIM + V_B
    w_packed = jnp.concatenate(
        [w[:, :c_ab], jnp.pad(w[:, c_ab:c_ab + 2 * N_HEADS_B], ((0, 0), (0, LANES - 2 * N_HEADS_B))),
         w[:, c_ab + 2 * N_HEADS_B:]], axis=1).astype(BF16)
    nmix = norm_mix[0][None, :]
    lane_pad = (0, LANES - N_HEADS_B)
    alog = jnp.pad(a_log[0], lane_pad)[None, :]
    dtb = jnp.pad(dt_bias[0], lane_pad)[None, :]
    gdnw = gdn_norm[0][None, :]
    cw = conv_w[0]
    ffn_w = (w_oa[0].astype(BF16), w_ob[0].astype(BF16), w_out[0].astype(BF16), norm_ffn[0][None, :],
             w_gate[0].astype(BF16), w_up[0].astype(BF16), w_down[0].astype(BF16), norm_final[None, :])
    sk = sinks[0]

    x_meta = jnp.concatenate([jnp.zeros((PAD_ROWS, D_MODEL), F32), meta_tokens.astype(F32)], axis=0)[None]
    cos_m, sin_m = _rope_tables(jnp.arange(BLK, dtype=jnp.int32) - PAD_ROWS)
    _, k_m, v_m, conv_m, z_m, ab_m, _ = _proj(x_meta, nmix, w_packed, cos_m, sin_m, tm=BLK)
    zero_tail = jnp.zeros((SUBLANES, CONV_DIM), F32)
    prep_m = _delta_prep(conv_m, ab_m, zero_tail, cw, alog, dtb, g_blocks=1)
    _, s_meta = _delta_seq(*prep_m, z_m, jnp.zeros((1, N_HEADS_B, DK, DK), F32), gdnw, group=1)

    cos_p, sin_p = _rope_tables(N_META + jnp.arange(seq, dtype=jnp.int32))
    q_p, k_p, v_p, conv_p, z_p, ab_p, gate_p = _proj(x_prompt, nmix, w_packed, cos_p, sin_p, tm=4 * BLK)
    oa_p = _attn_prompt(sk, q_p, k_p, v_p, k_m[0], v_m[0], g_blocks=4)
    prep_p = _delta_prep(conv_p, ab_p, conv_m[0, BLK - SUBLANES:], cw, alog, dtb, g_blocks=4)
    ob_p, s_p = _delta_seq(*prep_p, z_p, jnp.broadcast_to(s_meta, (nb, N_HEADS_B, DK, DK)), gdnw, group=4)
    y_prompt = _ffn(x_prompt, oa_p, ob_p, gate_p, *ffn_w, tm=4 * BLK)

    rows_s = ns * n_new
    pos_s = PAST_LEN + (jnp.arange(rows_s, dtype=jnp.int32) % n_new)
    cos_s, sin_s = _rope_tables(pos_s)
    outs = _proj(x_sample.reshape(1, rows_s, D_MODEL), nmix, w_packed, cos_s, sin_s, tm=rows_s // 2)
    q_s, k_s, v_s, conv_s, z_s, ab_s, gate_s = [o.reshape(ns, n_new, o.shape[-1]) for o in outs]
    pad8 = functools.partial(_pad_seq_rows, rows=SUBLANES)
    oa_s, k_cache, v_cache = _attn_sample(
        sk, pad8(q_s), pad8(k_s), pad8(v_s), cache_swa_k[0].reshape(ns, BLK, KV_A),
        cache_swa_v[0].reshape(ns, BLK, KV_A), n_new, sb=8)
    n_hist = CONV_W - 1
    assert n_hist + n_new <= SUBLANES and (ns * SUBLANES) % BLK == 0
    front = functools.partial(_pad_seq_rows, rows=SUBLANES, front=n_hist)
    conv_pk = pad8(jnp.concatenate([state_conv[0], conv_s], axis=1)).reshape(1, ns * SUBLANES, CONV_DIM)
    ab_pk = front(ab_s).reshape(1, ns * SUBLANES, LANES)
    z_pk = front(z_s).reshape(1, ns * SUBLANES, V_B)
    n_blk_s = ns * SUBLANES // BLK
    prep_s = _delta_prep(conv_pk, ab_pk, zero_tail, cw, alog, dtb, g_blocks=4 if n_blk_s % 4 == 0 else 1,
                         seg=SUBLANES, valid=(n_hist, n_hist + n_new))
    ob_pk, s_s = _delta_seq_packed(*prep_s, z_pk, state_delta[0], gdnw, seg=SUBLANES)
    ob_s = ob_pk.reshape(ns, SUBLANES, V_B)[:, n_hist:n_hist + n_new]
    y_sample = _ffn(x_sample.reshape(1, rows_s, D_MODEL), oa_s[:, :n_new].reshape(1, rows_s, Q_A),
                    ob_s.reshape(1, rows_s, V_B), gate_s.reshape(1, rows_s, 2 * D_MODEL),
                    *ffn_w, tm=rows_s // 2).reshape(ns, n_new, D_MODEL)

    kv_shape = (1, -1, BLK, N_KV_HEADS, HEAD_DIM)
    conv_new_s = jnp.concatenate([state_conv[0], conv_s], axis=1)[:, -(CONV_W - 1):]
    return (y_prompt, y_sample,
            k_p[:, seq - BLK:].reshape(kv_shape), v_p[:, seq - BLK:].reshape(kv_shape),
            s_p[None], conv_p[:, seq - (CONV_W - 1):][None],
            k_cache.reshape(kv_shape), v_cache.reshape(kv_shape), s_s[None], conv_new_s[None])
```

```python
import functools

import jax
import jax.numpy as jnp
from jax import lax
from jax.experimental import pallas as pl
from jax.experimental.pallas import tpu as pltpu

F32 = jnp.float32
BF16 = jnp.bfloat16

D_MODEL = 1024
N_META = 16
EPS = 1e-6
HEAD_DIM = 64
N_HEADS_A = 8
N_KV_HEADS = 2
BLK = 128
ROPE_THETA = 10000.0
PAST_LEN = 16384
DK = 128
N_HEADS_B = 4
CONV_W = 4
D_FF = 2816
Q_A = N_HEADS_A * HEAD_DIM
KV_A = N_KV_HEADS * HEAD_DIM
QK_B = N_HEADS_B * DK
V_B = N_HEADS_B * DK
CONV_DIM = 2 * QK_B + V_B
LANES = 128
SUBLANES = 8
PAD_ROWS = BLK - N_META
MXU_N = 256
FF_CHUNK = MXU_N
VMEM_LIMIT = 56 * 1024 * 1024

C_QKV = (0, 768)
C_CONV = (768, 2304)
C_ZAB = (2304, 2944)
C_GATE = (2944, 4992)
W_COLS = 4992

_ARB2 = pltpu.CompilerParams(dimension_semantics=("arbitrary", "arbitrary"), vmem_limit_bytes=VMEM_LIMIT)
_ARB1 = pltpu.CompilerParams(dimension_semantics=("arbitrary",), vmem_limit_bytes=VMEM_LIMIT)


def _dot(a, b):
    return jnp.dot(a, b, preferred_element_type=F32)


def _dot_nt(a, b):
    return lax.dot_general(a, b, (((1,), (1,)), ((), ())), preferred_element_type=F32)


def _bmm(a, b):
    return lax.dot_general(a, b, (((2,), (1,)), ((0,), (0,))), preferred_element_type=F32)


def _sigmoid(x):
    return 1.0 / (1.0 + jnp.exp(-x))


def _rms(x, w):
    return x * lax.rsqrt(jnp.mean(x * x, axis=-1, keepdims=True) + EPS) * w


def _conv_silu(tail, u, convw_ref, cols=slice(None)):
    n = u.shape[0]
    ext = jnp.concatenate([tail, u], axis=0)
    y = ext[5:5 + n] * convw_ref[0:1, cols]
    y = y + ext[6:6 + n] * convw_ref[1:2, cols]
    y = y + ext[7:7 + n] * convw_ref[2:3, cols]
    y = y + u * convw_ref[3:4, cols]
    return y * _sigmoid(y)


def _l2norm_rows(x):
    ss = jnp.broadcast_to(jnp.sum(x * x, axis=-1, keepdims=True), x.shape)
    return x * lax.rsqrt(ss + EPS)


def _store_qkv_b(cb, out_ref, c0=0):
    for g in range(cb.shape[1] // DK):
        col = c0 + g * DK
        x = cb[:, g * DK:(g + 1) * DK]
        if col < QK_B:
            x = _l2norm_rows(x) * (DK ** -0.5)
        elif col < 2 * QK_B:
            x = _l2norm_rows(x)
        out_ref[:, col:col + DK] = x


def _proj_kernel(x_ref, nw_ref, w_ref, cos_ref, sin_ref, *rest, fuse_conv):
    if fuse_conv:
        tail0_ref, convw_ref, q_ref, k_ref, v_ref, conv_ref, z_ref, ab_ref, gate_ref, tail_ref, tail_sc = rest
    else:
        q_ref, k_ref, v_ref, conv_ref, z_ref, ab_ref, gate_ref = rest
    tm = x_ref.shape[0]
    if fuse_conv:
        @pl.when(pl.program_id(1) == 0)
        def _():
            tail_sc[...] = tail0_ref[...]

    xn = _rms(x_ref[...], nw_ref[...]).astype(BF16)
    cos = cos_ref[...]
    sin = sin_ref[...]
    lane = lax.broadcasted_iota(jnp.int32, cos.shape, 1)
    first_half = (lane & (HEAD_DIM - 1)) < (HEAD_DIM // 2)

    def rope(t):
        rot = jnp.where(first_half, pltpu.roll(t, LANES - HEAD_DIM // 2, 1), pltpu.roll(t, HEAD_DIM // 2, 1))
        return t * cos + rot * sin

    def cols(base, c0, width=MXU_N):
        return _dot(xn, w_ref[:, base + c0:base + c0 + width])

    def conv_chunk(c0):
        pre = cols(C_CONV[0], c0)
        cs = slice(c0, c0 + MXU_N)
        if fuse_conv:
            _store_qkv_b(_conv_silu(tail_sc[:, cs], pre, convw_ref, cs), conv_ref, c0)
            tail_sc[:, cs] = pre[tm - SUBLANES:]
            tail_ref[:, cs] = pre[tm - SUBLANES:]
        else:
            conv_ref[:, cs] = pre

    def gate_chunk(c0):
        gate_ref[:, c0:c0 + MXU_N] = _sigmoid(cols(C_GATE[0], c0))

    def q_chunk(c0):
        t = cols(C_QKV[0], c0)
        for g in range(MXU_N // LANES):
            q_ref[:, c0 + g * LANES:c0 + (g + 1) * LANES] = rope(t[:, g * LANES:(g + 1) * LANES]) * (HEAD_DIM ** -0.5)

    def kv_chunk():
        t = cols(C_QKV[0], Q_A)
        k_ref[...] = rope(t[:, :KV_A])
        v_ref[...] = t[:, KV_A:]

    def z_chunk(c0):
        z_ref[:, c0:c0 + MXU_N] = cols(C_ZAB[0], c0)

    def ab_chunk():
        ab_ref[...] = cols(C_ZAB[0], V_B, LANES)

    others = ([functools.partial(gate_chunk, c) for c in range(0, 2 * D_MODEL, MXU_N)]
              + [functools.partial(q_chunk, c) for c in range(0, Q_A, MXU_N)] + [kv_chunk]
              + [functools.partial(z_chunk, c) for c in range(0, V_B, MXU_N)] + [ab_chunk])
    n_conv = CONV_DIM // MXU_N
    per_conv = -(-len(others) // n_conv)
    for ci in range(n_conv):
        conv_chunk(ci * MXU_N)
        for job in others[ci * per_conv:(ci + 1) * per_conv]:
            job()


def _proj(x, norm_w, w_packed, cos_t, sin_t, tm, conv=None):
    nb, t, _ = x.shape
    assert t % tm == 0

    def row(width):
        return pl.BlockSpec((None, tm, width), lambda b, i: (b, i, 0))

    def const(shape):
        return pl.BlockSpec(shape, lambda b, i: (0,) * len(shape))

    widths = (Q_A, KV_A, KV_A, CONV_DIM, V_B, LANES, 2 * D_MODEL)
    in_specs = [
        row(D_MODEL),
        const((1, D_MODEL)),
        pl.BlockSpec((D_MODEL, W_COLS), lambda b, i: (0, 0), pipeline_mode=pl.Buffered(1)),
        pl.BlockSpec((tm, LANES), lambda b, i: (i, 0)),
        pl.BlockSpec((tm, LANES), lambda b, i: (i, 0)),
    ]
    out_specs = [row(w) for w in widths]
    out_shape = [jax.ShapeDtypeStruct((nb, t, w), F32) for w in widths]
    args = [x, norm_w, w_packed, cos_t, sin_t]
    scratch = []
    if conv is not None:
        in_specs += [const((SUBLANES, CONV_DIM)), const((CONV_W, CONV_DIM))]
        args += list(conv)
        out_specs.append(pl.BlockSpec((None, None, SUBLANES, CONV_DIM), lambda b, i: (b, i, 0, 0)))
        out_shape.append(jax.ShapeDtypeStruct((nb, t // tm, SUBLANES, CONV_DIM), F32))
        scratch.append(pltpu.VMEM((SUBLANES, CONV_DIM), F32))
    return pl.pallas_call(
        functools.partial(_proj_kernel, fuse_conv=conv is not None),
        grid=(nb, t // tm),
        in_specs=in_specs,
        out_specs=out_specs,
        out_shape=out_shape,
        scratch_shapes=scratch,
        compiler_params=_ARB2,
        name="proj",
    )(*args)


def _attend(sinks_ref, items, tq):
    lane = lax.broadcasted_iota(jnp.int32, (BLK, LANES), 1)
    lo_k = lane < HEAD_DIM
    lo_q = lane[:tq] < HEAD_DIM
    rows = lax.broadcasted_iota(jnp.int32, (2 * tq, BLK), 0)
    qi = jnp.where(rows < tq, rows, rows - tq)
    kj = lax.broadcasted_iota(jnp.int32, (2 * tq, BLK), 1)
    mask_c = kj <= qi
    row1 = lax.broadcasted_iota(jnp.int32, (2 * tq, 1), 0)
    ones = jnp.ones((BLK, LANES), BF16)

    def dup(x):
        r = pltpu.roll(x, HEAD_DIM, 1)
        return jnp.where(lo_k, x, r).astype(BF16), jnp.where(lo_k, r, x).astype(BF16)

    chains = []
    for it, (q_of_group, kp, vp, kc, vc, mask_p) in enumerate(items):
        kdp, vdp, kdc, vdc = dup(kp), dup(vp), dup(kc), dup(vc)
        for j in range(Q_A // LANES):
            kv = j // 2
            q2 = q_of_group(j)
            qs = jnp.concatenate([jnp.where(lo_q, q2, 0.0), jnp.where(lo_q, 0.0, q2)], axis=0).astype(BF16)
            chains.append((it, j, qs, kdp[kv], vdp[kv], kdc[kv], vdc[kv], mask_p))
    scores = [(jnp.where(ch[7], _dot_nt(ch[2], ch[3]), -jnp.inf), jnp.where(mask_c, _dot_nt(ch[2], ch[5]), -jnp.inf))
              for ch in chains]
    probs = []
    for ch, (sp, sc) in zip(chains, scores):
        sk = jnp.where(row1 < tq, sinks_ref[2 * ch[1]], sinks_ref[2 * ch[1] + 1])
        m = jnp.maximum(jnp.max(jnp.maximum(sp, sc), axis=-1, keepdims=True), sk)
        probs.append((jnp.exp(sp - m), jnp.exp(sc - m), jnp.exp(sk - m)))
    outs = [[None] * (Q_A // LANES) for _ in items]
    for ch, (pp, pc, ps) in zip(chains, probs):
        num = _dot(pp.astype(BF16), ch[4]) + _dot(pc.astype(BF16), ch[6])
        den = _dot((pp + pc).astype(BF16), ones) + ps
        o = num / den
        outs[ch[0]][ch[1]] = jnp.where(lo_q, o[:tq], o[tq:])
    return outs


def _attn_prompt_kernel(sinks_ref, q_ref, kp_ref, vp_ref, kc_ref, vc_ref, km_ref, vm_ref, o_ref, *, g_blocks):
    first = pl.program_id(1) == 0
    pv = jnp.where(first, PAD_ROWS, 0)
    rows = lax.broadcasted_iota(jnp.int32, (2 * BLK, BLK), 0)
    qi = jnp.where(rows < BLK, rows, rows - BLK)
    kj = lax.broadcasted_iota(jnp.int32, (2 * BLK, BLK), 1)
    mask_p = kj >= qi
    items = []
    for jb in range(g_blocks):
        rs = slice(jb * BLK, (jb + 1) * BLK)
        if jb == 0:
            kp = jnp.where(first, km_ref[...], kp_ref[...])
            vp = jnp.where(first, vm_ref[...], vp_ref[...])
            mp = mask_p & (kj >= pv)
        else:
            ps = slice((jb - 1) * BLK, jb * BLK)
            kp, vp, mp = kc_ref[ps, :], vc_ref[ps, :], mask_p
        items.append((lambda j, rs=rs: q_ref[rs, j * LANES:(j + 1) * LANES], kp, vp, kc_ref[rs, :], vc_ref[rs, :], mp))
    outs = _attend(sinks_ref, items, BLK)
    for jb in range(g_blocks):
        for j in range(Q_A // LANES):
            o_ref[jb * BLK:(jb + 1) * BLK, j * LANES:(j + 1) * LANES] = outs[jb][j]


def _attn_prompt(sinks, q, k, v, k_meta, v_meta, g_blocks):
    nb, t, _ = q.shape
    rows = g_blocks * BLK
    assert t % rows == 0

    def cur(b, i):
        return (b, i, 0)

    def prev(b, i):
        return (b, jnp.maximum(i * g_blocks - 1, 0), 0)

    meta_spec = pl.BlockSpec((BLK, KV_A), lambda b, i: (0, 0))
    return pl.pallas_call(
        functools.partial(_attn_prompt_kernel, g_blocks=g_blocks),
        grid=(nb, t // rows),
        in_specs=[
            pl.BlockSpec(memory_space=pltpu.SMEM),
            pl.BlockSpec((None, rows, Q_A), cur),
            pl.BlockSpec((None, BLK, KV_A), prev),
            pl.BlockSpec((None, BLK, KV_A), prev),
            pl.BlockSpec((None, rows, KV_A), cur),
            pl.BlockSpec((None, rows, KV_A), cur),
            meta_spec, meta_spec,
        ],
        out_specs=pl.BlockSpec((None, rows, Q_A), cur),
        out_shape=jax.ShapeDtypeStruct((nb, t, Q_A), F32),
        compiler_params=_ARB2,
        name="attn_prompt",
    )(sinks, q, k, v, k, v, k_meta, v_meta)


def _attn_sample_kernel(sinks_ref, q_ref, kp_ref, vp_ref, kc_ref, vc_ref, o_ref, ko_ref, vo_ref, *, sb, tq, n_new):
    rows = lax.broadcasted_iota(jnp.int32, (2 * tq, BLK), 0)
    qi = jnp.where(rows < tq, rows, rows - tq)
    kj = lax.broadcasted_iota(jnp.int32, (2 * tq, BLK), 1)
    mask_p = kj >= qi

    def pad_rows(x):
        return jnp.concatenate([x, jnp.zeros((BLK - tq, LANES), F32)], axis=0)

    items = [(lambda j, s=s: q_ref[s, :, j * LANES:(j + 1) * LANES], kp_ref[s], vp_ref[s],
              pad_rows(kc_ref[s]), pad_rows(vc_ref[s]), mask_p) for s in range(sb)]
    outs = _attend(sinks_ref, items, tq)
    for s in range(sb):
        for j in range(Q_A // LANES):
            o_ref[s, :, j * LANES:(j + 1) * LANES] = outs[s][j]
        ko_ref[s, 0:BLK - n_new, :] = kp_ref[s, n_new:BLK, :]
        ko_ref[s, BLK - n_new:BLK, :] = kc_ref[s, 0:n_new, :]
        vo_ref[s, 0:BLK - n_new, :] = vp_ref[s, n_new:BLK, :]
        vo_ref[s, BLK - n_new:BLK, :] = vc_ref[s, 0:n_new, :]


def _attn_sample(sinks, q, k_new, v_new, k_past, v_past, n_new, sb):
    ns, tq, _ = q.shape
    assert ns % sb == 0

    def idx(g):
        return (g, 0, 0)

    cache = jax.ShapeDtypeStruct((ns, BLK, KV_A), F32)
    return pl.pallas_call(
        functools.partial(_attn_sample_kernel, sb=sb, tq=tq, n_new=n_new),
        grid=(ns // sb,),
        in_specs=[
            pl.BlockSpec(memory_space=pltpu.SMEM),
            pl.BlockSpec((sb, tq, Q_A), idx),
            pl.BlockSpec((sb, BLK, KV_A), idx),
            pl.BlockSpec((sb, BLK, KV_A), idx),
            pl.BlockSpec((sb, tq, KV_A), idx),
            pl.BlockSpec((sb, tq, KV_A), idx),
        ],
        out_specs=[pl.BlockSpec((sb, tq, Q_A), idx), pl.BlockSpec((sb, BLK, KV_A), idx),
                   pl.BlockSpec((sb, BLK, KV_A), idx)],
        out_shape=[jax.ShapeDtypeStruct((ns, tq, Q_A), F32), cache, cache],
        compiler_params=_ARB1,
        name="attn_sample",
    )(sinks, q, k_past, v_past, k_new, v_new)


def _unit_lower_inverse(n_mat, r, c, levels):
    x = (r == c).astype(F32)[None] - jnp.where((((r ^ c) == 1) & (r > c))[None], n_mat, 0.0)
    for lg in range(1, levels):
        same_2g = (r >> (lg + 1)) == (c >> (lg + 1))
        joins = same_2g & (((r >> lg) & 1) == 1) & (((c >> lg) & 1) == 0)
        e = jnp.where(joins[None], n_mat, 0.0).astype(BF16)
        xb = x.astype(BF16)
        x = x - _bmm(_bmm(xb, e).astype(BF16), xb)
    return x


def _decay_gates(ab, alog_ref, dtb_ref, lane):
    xg = ab + dtb_ref[...]
    softplus = jnp.maximum(xg, 0.0) + jnp.log1p(jnp.exp(-jnp.abs(xg)))
    return jnp.where(lane < N_HEADS_B, -jnp.exp(alog_ref[...]) * softplus, _sigmoid(ab))


def _segment_sums(x, sum_mat):
    hi = x.astype(BF16)
    r1 = x - hi.astype(F32)
    mid = r1.astype(BF16)
    lo = (r1 - mid.astype(F32)).astype(BF16)
    s = _dot(sum_mat, lo) + _dot(sum_mat, mid) + _dot(sum_mat, hi)
    return s[:BLK], s[BLK:]


def _delta_prep_kernel(*refs, g_blocks, seg, valid, do_conv):
    rows = g_blocks * BLK
    if do_conv:
        (conv_ref, tailp_ref, tailf_ref, convw_ref, ab_ref, alog_ref, dtb_ref,
         wq_ref, u0_ref, qk_ref, kdt_ref, cdec_ref, n_sc, rhs_sc, cb_sc) = refs
        tail = jnp.where(pl.program_id(1) == 0, tailf_ref[...], tailp_ref[...])
        _store_qkv_b(_conv_silu(tail, conv_ref[...], convw_ref), cb_sc)
        cb = cb_sc
    else:
        cb, ab_ref, alog_ref, dtb_ref, wq_ref, u0_ref, qk_ref, kdt_ref, cdec_ref, n_sc, rhs_sc = refs
    lane = lax.broadcasted_iota(jnp.int32, (rows, LANES), 1)
    gb = _decay_gates(ab_ref[...], alog_ref, dtb_ref, lane)

    r = lax.broadcasted_iota(jnp.int32, (BLK, BLK), 0)
    c = lax.broadcasted_iota(jnp.int32, (BLK, BLK), 1)
    incl = r >= c
    strict = r > c
    row_ok = None
    if seg < BLK:
        same_seg = (r & -seg) == (c & -seg)
        incl = incl & same_seg
        strict = strict & same_seg
    if valid is not None:
        pos_r = r & (seg - 1)
        row_ok = (pos_r >= valid[0]) & (pos_r < valid[1])
    chunk = (r & -seg) == (c & -seg)
    sum_mat = jnp.concatenate([jnp.where(incl, 1.0, 0.0), jnp.where(chunk, 1.0, 0.0)],
                              axis=0).astype(BF16)
    for j in range(g_blocks):
        rs = slice(j * BLK, (j + 1) * BLK)
        gbj = gb[rs]
        if row_ok is not None:
            gbj = jnp.where(row_ok, gbj, 0.0)
        gc_cols, tot_cols = _segment_sums(gbj, sum_mat)
        gc_rows = gc_cols.T[0:SUBLANES]
        cdec_ref[j] = jnp.exp(tot_cols.T[0:SUBLANES])
        for h in range(N_HEADS_B):
            qh = cb[rs, h * DK:(h + 1) * DK]
            kh = cb[rs, QK_B + h * DK:QK_B + (h + 1) * DK]
            vh = cb[rs, 2 * QK_B + h * DK:2 * QK_B + (h + 1) * DK]
            if row_ok is not None:
                qh = jnp.where(row_ok, qh, 0.0)
                kh = jnp.where(row_ok, kh, 0.0)
            beta = jnp.broadcast_to(gbj[:, N_HEADS_B + h:N_HEADS_B + h + 1], (BLK, DK))
            gcc = jnp.broadcast_to(gc_cols[:, h:h + 1], (BLK, DK))
            gct = jnp.broadcast_to(tot_cols[:, h:h + 1], (BLK, DK))
            gcr = gc_rows[h:h + 1, :]
            decay = jnp.where(incl, jnp.exp(jnp.where(incl, gcc - gcr, 0.0)), 0.0)
            eg = jnp.exp(gcc)
            kb = kh.astype(BF16)
            n_sc[j * N_HEADS_B + h] = jnp.where(strict, beta * decay * _dot_nt(kb, kb), 0.0)
            rhs_sc[j * N_HEADS_B + h] = jnp.concatenate([beta * eg * kh, beta * vh], axis=1).astype(BF16)
            qk_ref[j, h] = (_dot_nt(qh.astype(BF16), kb) * decay).astype(BF16)
            wq_ref[j, h, BLK:2 * BLK, :] = (qh * eg).astype(BF16)
            kdt_ref[j, h] = (kh * jnp.exp(gct - gcc)).T.astype(BF16)

    t_inv = _unit_lower_inverse(n_sc[...], r, c, levels=seg.bit_length() - 1)
    wu = _bmm(t_inv.astype(BF16), rhs_sc[...])
    for j in range(g_blocks):
        for h in range(N_HEADS_B):
            wq_ref[j, h, 0:BLK, :] = wu[j * N_HEADS_B + h, :, :DK].astype(BF16)
            u0_ref[j, h] = wu[j * N_HEADS_B + h, :, DK:]


def _delta_prep(qkv_b, ab, alog, dtb, g_blocks, seg=BLK, valid=None, conv=None):
    nb, t, _ = qkv_b.shape
    n_blk = t // BLK
    assert n_blk % g_blocks == 0
    rows = g_blocks * BLK
    chains = g_blocks * N_HEADS_B

    def const(shape):
        return pl.BlockSpec(shape, lambda b, j: (0,) * len(shape))

    def out(*tail_shape):
        return pl.BlockSpec((None, g_blocks) + tail_shape, lambda b, j: (b, j) + (0,) * len(tail_shape))

    in_specs = [pl.BlockSpec((None, rows, CONV_DIM), lambda b, j: (b, j, 0))]
    args = [qkv_b]
    scratch = [pltpu.VMEM((chains, BLK, BLK), F32), pltpu.VMEM((chains, BLK, 2 * DK), BF16)]
    if conv is not None:
        in_specs += [pl.BlockSpec((None, SUBLANES, CONV_DIM),
                                  lambda b, j: (b, jnp.maximum(j * (rows // SUBLANES) - 1, 0), 0)),
                     const((SUBLANES, CONV_DIM)), const((CONV_W, CONV_DIM))]
        args += [qkv_b, conv[0], conv[1]]
        scratch += [pltpu.VMEM((rows, CONV_DIM), F32)]
    in_specs += [pl.BlockSpec((None, rows, LANES), lambda b, j: (b, j, 0)), const((1, LANES)), const((1, LANES))]
    args += [ab, alog, dtb]
    kern = functools.partial(_delta_prep_kernel, g_blocks=g_blocks, seg=seg, valid=valid, do_conv=conv is not None)
    return pl.pallas_call(
        kern,
        grid=(nb, n_blk // g_blocks),
        in_specs=in_specs,
        out_specs=[out(N_HEADS_B, 2 * BLK, DK), out(N_HEADS_B, BLK, DK), out(N_HEADS_B, BLK, BLK),
                   out(N_HEADS_B, DK, BLK), out(SUBLANES, BLK)],
        out_shape=[
            jax.ShapeDtypeStruct((nb, n_blk, N_HEADS_B, 2 * BLK, DK), BF16),
            jax.ShapeDtypeStruct((nb, n_blk, N_HEADS_B, BLK, DK), F32),
            jax.ShapeDtypeStruct((nb, n_blk, N_HEADS_B, BLK, BLK), BF16),
            jax.ShapeDtypeStruct((nb, n_blk, N_HEADS_B, DK, BLK), BF16),
            jax.ShapeDtypeStruct((nb, n_blk, SUBLANES, BLK), F32),
        ],
        scratch_shapes=scratch,
        compiler_params=_ARB2,
        name="delta_prep",
    )(*args)


def _gated_norm(o, z, gdnw):
    return _rms(o, gdnw) * (z * _sigmoid(z))


def _delta_seq_kernel(wq_ref, u0_ref, qk_ref, kdt_ref, cdec_ref, z_ref, s0_ref, gdnw_ref,
                      o_ref, sout_ref, s_sc, *, nbs, group):
    i = pl.program_id(0)

    @pl.when(i == 0)
    def _():
        s_sc[...] = s0_ref[...]

    gdnw = gdnw_ref[...]
    for b0 in range(0, nbs, group):
        chains = [(b, h) for b in range(b0, min(b0 + group, nbs)) for h in range(N_HEADS_B)]
        s_old = [s_sc[b, h] for b, h in chains]
        ws_qs = [_dot(wq_ref[b, h], s.astype(BF16)) for (b, h), s in zip(chains, s_old)]
        u_new = [(u0_ref[b, h] - x[:BLK]).astype(BF16) for (b, h), x in zip(chains, ws_qs)]
        for (b, h), s, x, u in zip(chains, s_old, ws_qs, u_new):
            o = x[BLK:] + _dot(qk_ref[b, h], u)
            s_sc[b, h] = cdec_ref[b, h:h + 1, :] * s + _dot(kdt_ref[b, h], u)
            o_ref[b, :, h * DK:(h + 1) * DK] = _gated_norm(o, z_ref[b, :, h * DK:(h + 1) * DK], gdnw)

    @pl.when(i == pl.num_programs(0) - 1)
    def _():
        sout_ref[...] = s_sc[...]


def _delta_seq(wq, u0, qk, kdt, cdec, z, s0, gdnw, group):
    nbs, n_blk = wq.shape[:2]

    def blk(*tail_shape):
        return pl.BlockSpec((nbs, None) + tail_shape, lambda i: (0, i) + (0,) * len(tail_shape))

    state = pl.BlockSpec((nbs, N_HEADS_B, DK, DK), lambda i: (0, 0, 0, 0))
    kern = functools.partial(_delta_seq_kernel, nbs=nbs, group=group)
    return pl.pallas_call(
        kern,
        grid=(n_blk,),
        in_specs=[
            blk(N_HEADS_B, 2 * BLK, DK), blk(N_HEADS_B, BLK, DK), blk(N_HEADS_B, BLK, BLK),
            blk(N_HEADS_B, DK, BLK), blk(SUBLANES, BLK),
            pl.BlockSpec((nbs, BLK, V_B), lambda i: (0, i, 0)),
            state,
            pl.BlockSpec((1, DK), lambda i: (0, 0)),
        ],
        out_specs=[pl.BlockSpec((nbs, BLK, V_B), lambda i: (0, i, 0)), state],
        out_shape=[jax.ShapeDtypeStruct((nbs, n_blk * BLK, V_B), F32),
                   jax.ShapeDtypeStruct((nbs, N_HEADS_B, DK, DK), F32)],
        scratch_shapes=[pltpu.VMEM((nbs, N_HEADS_B, DK, DK), F32)],
        compiler_params=_ARB1,
        name="delta_seq",
    )(wq, u0, qk, kdt, cdec, z, s0, gdnw)


def _delta_seq_packed_kernel(wq_ref, u0_ref, qk_ref, kdt_ref, cdec_ref, z_ref, s0_ref, gdnw_ref,
                             o_ref, sout_ref, *, seg):
    n_seq = BLK // seg
    tile = 2 * SUBLANES
    r = lax.broadcasted_iota(jnp.int32, (BLK, DK), 0)
    gdnw = gdnw_ref[...]
    for h in range(N_HEADS_B):
        ws_parts, qs_parts = [], []
        for s in range(n_seq):
            t0 = (s * seg) // tile * tile
            off = s * seg - t0
            lhs = jnp.concatenate([wq_ref[h, t0:t0 + tile, :], wq_ref[h, BLK + t0:BLK + t0 + tile, :]], axis=0)
            x = _dot(lhs, s0_ref[s, h].astype(BF16))
            ws_parts.append(x[off:off + seg])
            qs_parts.append(x[tile + off:tile + off + seg])
        u_new = u0_ref[h] - jnp.concatenate(ws_parts, axis=0)
        o = jnp.concatenate(qs_parts, axis=0) + _dot(qk_ref[h], u_new.astype(BF16))
        o_ref[:, h * DK:(h + 1) * DK] = _gated_norm(o, z_ref[:, h * DK:(h + 1) * DK], gdnw)
        kdt = kdt_ref[h]
        for s in range(n_seq):
            own = (r >= s * seg) & (r < (s + 1) * seg)
            u_own = jnp.where(own, u_new, 0.0).astype(BF16)
            sout_ref[s, h] = cdec_ref[h:h + 1, s * seg:s * seg + 1] * s0_ref[s, h] + _dot(kdt, u_own)


def _delta_seq_packed(wq, u0, qk, kdt, cdec, z, s0, gdnw, seg):
    n_blk = wq.shape[1]
    n_seq = BLK // seg
    assert wq.shape[0] == 1 and s0.shape[0] == n_blk * n_seq

    def blk(*tail_shape):
        return pl.BlockSpec((None, None) + tail_shape, lambda g: (0, g) + (0,) * len(tail_shape))

    state = pl.BlockSpec((n_seq, N_HEADS_B, DK, DK), lambda g: (g, 0, 0, 0))
    rows = pl.BlockSpec((None, BLK, V_B), lambda g: (0, g, 0))
    return pl.pallas_call(
        functools.partial(_delta_seq_packed_kernel, seg=seg),
        grid=(n_blk,),
        in_specs=[
            blk(N_HEADS_B, 2 * BLK, DK), blk(N_HEADS_B, BLK, DK), blk(N_HEADS_B, BLK, BLK),
            blk(N_HEADS_B, DK, BLK), blk(SUBLANES, BLK), rows, state,
            pl.BlockSpec((1, DK), lambda g: (0, 0)),
        ],
        out_specs=[rows, state],
        out_shape=[jax.ShapeDtypeStruct((1, n_blk * BLK, V_B), F32),
                   jax.ShapeDtypeStruct(s0.shape, F32)],
        compiler_params=_ARB1,
        name="delta_seq_packed",
    )(wq, u0, qk, kdt, cdec, z, s0, gdnw)


def _ffn_kernel(x_ref, oa_ref, ob_ref, g_ref, woa_ref, wob_ref, wout_ref, nffn_ref, wg_ref, wu_ref, wd_ref,
                nfin_ref, y_ref, acc_ref):
    ga = g_ref[:, :D_MODEL]
    gb = g_ref[:, D_MODEL:]
    merged = ga * _dot(oa_ref[...].astype(BF16), woa_ref[...]) + gb * _dot(ob_ref[...].astype(BF16), wob_ref[...])
    h = x_ref[...] + _dot(merged.astype(BF16), wout_ref[...])
    hn = _rms(h, nffn_ref[...]).astype(BF16)
    acc_ref[...] = h
    for c0 in range(0, D_FF, FF_CHUNK):
        gate = _dot(hn, wg_ref[:, c0:c0 + FF_CHUNK])
        up = _dot(hn, wu_ref[:, c0:c0 + FF_CHUNK])
        act = (gate * _sigmoid(gate) * up).astype(BF16)
        acc_ref[...] += _dot(act, wd_ref[c0:c0 + FF_CHUNK, :])
    y_ref[...] = _rms(acc_ref[...], nfin_ref[...])


def _ffn(x, oa, ob, gates, w_oa, w_ob, w_out, norm_ffn, w_gate, w_up, w_down, norm_final, tm):
    nb, t, _ = x.shape
    assert t % tm == 0

    def row(width):
        return pl.BlockSpec((None, tm, width), lambda b, i: (b, i, 0))

    def const(shape):
        return pl.BlockSpec(shape, lambda b, i: (0, 0), pipeline_mode=pl.Buffered(1))

    return pl.pallas_call(
        _ffn_kernel,
        grid=(nb, t // tm),
        in_specs=[
            row(D_MODEL), row(Q_A), row(V_B), row(2 * D_MODEL),
            const((Q_A, D_MODEL)), const((V_B, D_MODEL)), const((D_MODEL, D_MODEL)), const((1, D_MODEL)),
            const((D_MODEL, D_FF)), const((D_MODEL, D_FF)), const((D_FF, D_MODEL)), const((1, D_MODEL)),
        ],
        out_specs=row(D_MODEL),
        out_shape=jax.ShapeDtypeStruct((nb, t, D_MODEL), F32),
        scratch_shapes=[pltpu.VMEM((tm, D_MODEL), F32)],
        compiler_params=_ARB2,
        name="ffn",
    )(x, oa, ob, gates, w_oa, w_ob, w_out, norm_ffn, w_gate, w_up, w_down, norm_final)


def _rope_tables(pos):
    half = HEAD_DIM // 2
    inv = ROPE_THETA ** (-jnp.arange(half, dtype=F32) / half)
    ang = pos.astype(F32)[:, None] * inv[None, :]
    cos, sin = jnp.cos(ang), jnp.sin(ang)
    reps = LANES // HEAD_DIM
    return (jnp.tile(jnp.concatenate([cos, cos], axis=1), (1, reps)),
            jnp.tile(jnp.concatenate([-sin, sin], axis=1), (1, reps)))


def _pad_seq_rows(x, rows, front=0):
    return jnp.pad(x, ((0, 0), (front, rows - front - x.shape[1]), (0, 0)))


def kernel(x_prompt, x_sample, cache_swa_k, cache_swa_v, state_delta, state_conv, meta_tokens, norm_mix, w_in,
           conv_w, a_log, dt_bias, sinks, gdn_norm, w_oa, w_ob, w_out, norm_ffn, w_gate, w_up, w_down,
           norm_final):
    assert w_in.shape[0] == 1, "single trunk layer"
    nb, seq, _ = x_prompt.shape
    ns, n_new, _ = x_sample.shape
    assert seq % BLK == 0 and n_new <= SUBLANES

    w = w_in[0]
    c_ab = Q_A + 2 * KV_A + CONV_DIM + V_B
    w_packed = jnp.concatenate(
        [w[:, :c_ab], jnp.pad(w[:, c_ab:c_ab + 2 * N_HEADS_B], ((0, 0), (0, LANES - 2 * N_HEADS_B))),
         w[:, c_ab + 2 * N_HEADS_B:]], axis=1).astype(BF16)
    nmix = norm_mix[0][None, :]
    lane_pad = (0, LANES - N_HEADS_B)
    alog = jnp.pad(a_log[0], lane_pad)[None, :]
    dtb = jnp.pad(dt_bias[0], lane_pad)[None, :]
    gdnw = gdn_norm[0][None, :]
    cw = conv_w[0]
    ffn_w = (w_oa[0].astype(BF16), w_ob[0].astype(BF16), w_out[0].astype(BF16), norm_ffn[0][None, :],
             w_gate[0].astype(BF16), w_up[0].astype(BF16), w_down[0].astype(BF16), norm_final[None, :])
    sk = sinks[0]

    x_meta = jnp.concatenate([jnp.zeros((PAD_ROWS, D_MODEL), F32), meta_tokens.astype(F32)], axis=0)[None]
    cos_m, sin_m = _rope_tables(jnp.arange(BLK, dtype=jnp.int32) - PAD_ROWS)
    zero_tail = jnp.zeros((SUBLANES, CONV_DIM), F32)
    _, k_m, v_m, qkvb_m, z_m, ab_m, _, tail_m = _proj(x_meta, nmix, w_packed, cos_m, sin_m, tm=BLK,
                                                     conv=(zero_tail, cw))
    prep_m = _delta_prep(qkvb_m, ab_m, alog, dtb, g_blocks=1)
    _, s_meta = _delta_seq(*prep_m, z_m, jnp.zeros((1, N_HEADS_B, DK, DK), F32), gdnw, group=1)

    cos_p, sin_p = _rope_tables(N_META + jnp.arange(seq, dtype=jnp.int32))
    q_p, k_p, v_p, qkvb_p, z_p, ab_p, gate_p, tail_p = _proj(x_prompt, nmix, w_packed, cos_p, sin_p, tm=4 * BLK,
                                                            conv=(tail_m[0, 0], cw))
    oa_p = _attn_prompt(sk, q_p, k_p, v_p, k_m[0], v_m[0], g_blocks=4)
    prep_p = _delta_prep(qkvb_p, ab_p, alog, dtb, g_blocks=4)
    ob_p, s_p = _delta_seq(*prep_p, z_p, jnp.broadcast_to(s_meta, (nb, N_HEADS_B, DK, DK)), gdnw, group=4)
    y_prompt = _ffn(x_prompt, oa_p, ob_p, gate_p, *ffn_w, tm=4 * BLK)

    rows_s = ns * n_new
    pos_s = PAST_LEN + (jnp.arange(rows_s, dtype=jnp.int32) % n_new)
    cos_s, sin_s = _rope_tables(pos_s)
    outs = _proj(x_sample.reshape(1, rows_s, D_MODEL), nmix, w_packed, cos_s, sin_s, tm=rows_s // 2)
    q_s, k_s, v_s, conv_s, z_s, ab_s, gate_s = [o.reshape(ns, n_new, o.shape[-1]) for o in outs]
    pad8 = functools.partial(_pad_seq_rows, rows=SUBLANES)
    oa_s, k_cache, v_cache = _attn_sample(
        sk, pad8(q_s), pad8(k_s), pad8(v_s), cache_swa_k[0].reshape(ns, BLK, KV_A),
        cache_swa_v[0].reshape(ns, BLK, KV_A), n_new, sb=8)
    n_hist = CONV_W - 1
    assert n_hist + n_new <= SUBLANES and (ns * SUBLANES) % BLK == 0
    front = functools.partial(_pad_seq_rows, rows=SUBLANES, front=n_hist)
    conv_pk = pad8(jnp.concatenate([state_conv[0], conv_s], axis=1)).reshape(1, ns * SUBLANES, CONV_DIM)
    ab_pk = front(ab_s).reshape(1, ns * SUBLANES, LANES)
    z_pk = front(z_s).reshape(1, ns * SUBLANES, V_B)
    n_blk_s = ns * SUBLANES // BLK
    prep_s = _delta_prep(conv_pk, ab_pk, alog, dtb, g_blocks=4 if n_blk_s % 4 == 0 else 1,
                         seg=SUBLANES, valid=(n_hist, n_hist + n_new), conv=(zero_tail, cw))
    ob_pk, s_s = _delta_seq_packed(*prep_s, z_pk, state_delta[0], gdnw, seg=SUBLANES)
    ob_s = ob_pk.reshape(ns, SUBLANES, V_B)[:, n_hist:n_hist + n_new]
    y_sample = _ffn(x_sample.reshape(1, rows_s, D_MODEL), oa_s[:, :n_new].reshape(1, rows_s, Q_A),
                    ob_s.reshape(1, rows_s, V_B), gate_s.reshape(1, rows_s, 2 * D_MODEL),
                    *ffn_w, tm=rows_s // 2).reshape(ns, n_new, D_MODEL)

    kv_shape = (1, -1, BLK, N_KV_HEADS, HEAD_DIM)
    conv_new_s = jnp.concatenate([state_conv[0], conv_s], axis=1)[:, -(CONV_W - 1):]
    return (y_prompt, y_sample,
            k_p[:, seq - BLK:].reshape(kv_shape), v_p[:, seq - BLK:].reshape(kv_shape),
            s_p[None], tail_p[:, -1, SUBLANES - (CONV_W - 1):][None],
            k_cache.reshape(kv_shape), v_cache.reshape(kv_shape), s_s[None], conv_new_s[None])
```

```python
import functools

import jax
import jax.numpy as jnp
from jax import lax
from jax.experimental import pallas as pl
from jax.experimental.pallas import tpu as pltpu

F32 = jnp.float32
BF16 = jnp.bfloat16

D_MODEL = 1024
N_META = 16
EPS = 1e-6
HEAD_DIM = 64
N_HEADS_A = 8
N_KV_HEADS = 2
BLK = 128
ROPE_THETA = 10000.0
PAST_LEN = 16384
DK = 128
N_HEADS_B = 4
CONV_W = 4
D_FF = 2816
Q_A = N_HEADS_A * HEAD_DIM
KV_A = N_KV_HEADS * HEAD_DIM
QK_B = N_HEADS_B * DK
V_B = N_HEADS_B * DK
CONV_DIM = 2 * QK_B + V_B
LANES = 128
SUBLANES = 8
PAD_ROWS = BLK - N_META
MXU_N = 256
FF_CHUNK = MXU_N
VMEM_LIMIT = 56 * 1024 * 1024

C_QKV = 0
C_CONV = 768
C_Z = 2304
W_MAIN_COLS = 2816

_ARB2 = pltpu.CompilerParams(dimension_semantics=("arbitrary", "arbitrary"), vmem_limit_bytes=VMEM_LIMIT)
_ARB1 = pltpu.CompilerParams(dimension_semantics=("arbitrary",), vmem_limit_bytes=VMEM_LIMIT)


def _dot(a, b):
    return jnp.dot(a, b, preferred_element_type=F32)


def _dot_nt(a, b):
    return lax.dot_general(a, b, (((1,), (1,)), ((), ())), preferred_element_type=F32)


def _bmm(a, b):
    return lax.dot_general(a, b, (((2,), (1,)), ((0,), (0,))), preferred_element_type=F32)


def _sigmoid(x):
    return 1.0 / (1.0 + jnp.exp(-x))


def _rms(x, w):
    return x * lax.rsqrt(jnp.mean(x * x, axis=-1, keepdims=True) + EPS) * w


def _conv_silu(tail, u, convw_ref, cols=slice(None)):
    n = u.shape[0]
    ext = jnp.concatenate([tail, u], axis=0)
    y = ext[5:5 + n] * convw_ref[0:1, cols]
    y = y + ext[6:6 + n] * convw_ref[1:2, cols]
    y = y + ext[7:7 + n] * convw_ref[2:3, cols]
    y = y + u * convw_ref[3:4, cols]
    return y * _sigmoid(y)


def _l2norm_rows(x):
    ss = jnp.broadcast_to(jnp.sum(x * x, axis=-1, keepdims=True), x.shape)
    return x * lax.rsqrt(ss + EPS)


def _store_qkv_b(cb, out_ref, c0=0):
    for g in range(cb.shape[1] // DK):
        col = c0 + g * DK
        x = cb[:, g * DK:(g + 1) * DK]
        if col < QK_B:
            x = _l2norm_rows(x) * (DK ** -0.5)
        elif col < 2 * QK_B:
            x = _l2norm_rows(x)
        out_ref[:, col:col + DK] = x


def _proj_kernel(x_ref, nw_ref, w_ref, wab_ref, wgate_ref, cos_ref, sin_ref, *rest, fuse_conv):
    if fuse_conv:
        tail0_ref, convw_ref, q_ref, k_ref, v_ref, conv_ref, z_ref, ab_ref, gate_ref, tail_ref, tail_sc = rest
    else:
        q_ref, k_ref, v_ref, conv_ref, z_ref, ab_ref, gate_ref = rest
    tm = x_ref.shape[0]
    if fuse_conv:
        @pl.when(pl.program_id(1) == 0)
        def _():
            tail_sc[...] = tail0_ref[...]

    xn = _rms(x_ref[...], nw_ref[...]).astype(BF16)
    cos = cos_ref[...]
    sin = sin_ref[...]
    lane = lax.broadcasted_iota(jnp.int32, cos.shape, 1)
    first_half = (lane & (HEAD_DIM - 1)) < (HEAD_DIM // 2)

    def rope(t):
        rot = jnp.where(first_half, pltpu.roll(t, LANES - HEAD_DIM // 2, 1), pltpu.roll(t, HEAD_DIM // 2, 1))
        return t * cos + rot * sin

    def cols(base, c0, width=MXU_N):
        return _dot(xn, w_ref[:, base + c0:base + c0 + width])

    def conv_chunk(c0):
        pre = cols(C_CONV, c0)
        cs = slice(c0, c0 + MXU_N)
        if fuse_conv:
            _store_qkv_b(_conv_silu(tail_sc[:, cs], pre, convw_ref, cs), conv_ref, c0)
            tail_sc[:, cs] = pre[tm - SUBLANES:]
            tail_ref[:, cs] = pre[tm - SUBLANES:]
        else:
            conv_ref[:, cs] = pre

    def gate_chunk(c0):
        gate_ref[:, c0:c0 + MXU_N] = _dot(xn, wgate_ref[:, c0:c0 + MXU_N])

    def q_chunk(c0):
        t = cols(C_QKV, c0)
        for g in range(MXU_N // LANES):
            q_ref[:, c0 + g * LANES:c0 + (g + 1) * LANES] = rope(t[:, g * LANES:(g + 1) * LANES]) * (HEAD_DIM ** -0.5)

    def kv_chunk():
        t = cols(C_QKV, Q_A)
        k_ref[...] = rope(t[:, :KV_A])
        v_ref[...] = t[:, KV_A:]

    def z_chunk(c0):
        z_ref[:, c0:c0 + MXU_N] = cols(C_Z, c0)

    def ab_chunk():
        ab_ref[...] = _dot(xn, wab_ref[...])

    others = ([functools.partial(gate_chunk, c) for c in range(0, 2 * D_MODEL, MXU_N)]
              + [functools.partial(q_chunk, c) for c in range(0, Q_A, MXU_N)] + [kv_chunk]
              + [functools.partial(z_chunk, c) for c in range(0, V_B, MXU_N)] + [ab_chunk])
    n_conv = CONV_DIM // MXU_N
    per_conv = -(-len(others) // n_conv)
    for ci in range(n_conv):
        conv_chunk(ci * MXU_N)
        for job in others[ci * per_conv:(ci + 1) * per_conv]:
            job()


def _proj(x, norm_w, w_parts, cos_t, sin_t, tm, conv=None):
    nb, t, _ = x.shape
    assert t % tm == 0

    def row(width):
        return pl.BlockSpec((None, tm, width), lambda b, i: (b, i, 0))

    def const(shape):
        return pl.BlockSpec(shape, lambda b, i: (0,) * len(shape))

    widths = (Q_A, KV_A, KV_A, CONV_DIM, V_B, LANES, 2 * D_MODEL)
    in_specs = [
        row(D_MODEL),
        const((1, D_MODEL)),
        pl.BlockSpec((D_MODEL, W_MAIN_COLS), lambda b, i: (0, 0), pipeline_mode=pl.Buffered(1)),
        pl.BlockSpec((D_MODEL, LANES), lambda b, i: (0, 0), pipeline_mode=pl.Buffered(1)),
        pl.BlockSpec((D_MODEL, 2 * D_MODEL), lambda b, i: (0, 0), pipeline_mode=pl.Buffered(1)),
        pl.BlockSpec((tm, LANES), lambda b, i: (i, 0)),
        pl.BlockSpec((tm, LANES), lambda b, i: (i, 0)),
    ]
    out_specs = [row(w) for w in widths]
    out_shape = [jax.ShapeDtypeStruct((nb, t, w), F32) for w in widths]
    args = [x, norm_w, *w_parts, cos_t, sin_t]
    scratch = []
    if conv is not None:
        in_specs += [const((SUBLANES, CONV_DIM)), const((CONV_W, CONV_DIM))]
        args += list(conv)
        out_specs.append(pl.BlockSpec((None, None, SUBLANES, CONV_DIM), lambda b, i: (b, i, 0, 0)))
        out_shape.append(jax.ShapeDtypeStruct((nb, t // tm, SUBLANES, CONV_DIM), F32))
        scratch.append(pltpu.VMEM((SUBLANES, CONV_DIM), F32))
    return pl.pallas_call(
        functools.partial(_proj_kernel, fuse_conv=conv is not None),
        grid=(nb, t // tm),
        in_specs=in_specs,
        out_specs=out_specs,
        out_shape=out_shape,
        scratch_shapes=scratch,
        compiler_params=_ARB2,
        name="proj",
    )(*args)


def _attend(sinks_ref, items, tq):
    lane = lax.broadcasted_iota(jnp.int32, (BLK, LANES), 1)
    lo_k = lane < HEAD_DIM
    lo_q = lane[:tq] < HEAD_DIM
    rows = lax.broadcasted_iota(jnp.int32, (2 * tq, BLK), 0)
    qi = jnp.where(rows < tq, rows, rows - tq)
    kj = lax.broadcasted_iota(jnp.int32, (2 * tq, BLK), 1)
    mask_c = kj <= qi
    row1 = lax.broadcasted_iota(jnp.int32, (2 * tq, 1), 0)
    ones = jnp.ones((BLK, LANES), BF16)

    def dup(x):
        r = pltpu.roll(x, HEAD_DIM, 1)
        return jnp.where(lo_k, x, r).astype(BF16), jnp.where(lo_k, r, x).astype(BF16)

    chains = []
    for it, (q_of_group, kp, vp, kc, vc, mask_p) in enumerate(items):
        kdp, vdp, kdc, vdc = dup(kp), dup(vp), dup(kc), dup(vc)
        for j in range(Q_A // LANES):
            kv = j // 2
            q2 = q_of_group(j)
            qs = jnp.concatenate([jnp.where(lo_q, q2, 0.0), jnp.where(lo_q, 0.0, q2)], axis=0).astype(BF16)
            chains.append((it, j, qs, kdp[kv], vdp[kv], kdc[kv], vdc[kv], mask_p))
    scores = [(jnp.where(ch[7], _dot_nt(ch[2], ch[3]), -jnp.inf), jnp.where(mask_c, _dot_nt(ch[2], ch[5]), -jnp.inf))
              for ch in chains]
    probs = []
    for ch, (sp, sc) in zip(chains, scores):
        sk = jnp.where(row1 < tq, sinks_ref[2 * ch[1]], sinks_ref[2 * ch[1] + 1])
        m = jnp.maximum(jnp.max(jnp.maximum(sp, sc), axis=-1, keepdims=True), sk)
        probs.append((jnp.exp(sp - m), jnp.exp(sc - m), jnp.exp(sk - m)))
    outs = [[None] * (Q_A // LANES) for _ in items]
    for ch, (pp, pc, ps) in zip(chains, probs):
        num = _dot(pp.astype(BF16), ch[4]) + _dot(pc.astype(BF16), ch[6])
        den = _dot((pp + pc).astype(BF16), ones) + ps
        o = num / den
        outs[ch[0]][ch[1]] = jnp.where(lo_q, o[:tq], o[tq:])
    return outs


def _attn_prompt_kernel(sinks_ref, q_ref, kp_ref, vp_ref, kc_ref, vc_ref, km_ref, vm_ref, o_ref, *, g_blocks):
    first = pl.program_id(1) == 0
    pv = jnp.where(first, PAD_ROWS, 0)
    rows = lax.broadcasted_iota(jnp.int32, (2 * BLK, BLK), 0)
    qi = jnp.where(rows < BLK, rows, rows - BLK)
    kj = lax.broadcasted_iota(jnp.int32, (2 * BLK, BLK), 1)
    mask_p = kj >= qi
    items = []
    for jb in range(g_blocks):
        rs = slice(jb * BLK, (jb + 1) * BLK)
        if jb == 0:
            kp = jnp.where(first, km_ref[...], kp_ref[...])
            vp = jnp.where(first, vm_ref[...], vp_ref[...])
            mp = mask_p & (kj >= pv)
        else:
            ps = slice((jb - 1) * BLK, jb * BLK)
            kp, vp, mp = kc_ref[ps, :], vc_ref[ps, :], mask_p
        items.append((lambda j, rs=rs: q_ref[rs, j * LANES:(j + 1) * LANES], kp, vp, kc_ref[rs, :], vc_ref[rs, :], mp))
    outs = _attend(sinks_ref, items, BLK)
    for jb in range(g_blocks):
        for j in range(Q_A // LANES):
            o_ref[jb * BLK:(jb + 1) * BLK, j * LANES:(j + 1) * LANES] = outs[jb][j]


def _attn_prompt(sinks, q, k, v, k_meta, v_meta, g_blocks):
    nb, t, _ = q.shape
    rows = g_blocks * BLK
    assert t % rows == 0

    def cur(b, i):
        return (b, i, 0)

    def prev(b, i):
        return (b, jnp.maximum(i * g_blocks - 1, 0), 0)

    meta_spec = pl.BlockSpec((BLK, KV_A), lambda b, i: (0, 0))
    return pl.pallas_call(
        functools.partial(_attn_prompt_kernel, g_blocks=g_blocks),
        grid=(nb, t // rows),
        in_specs=[
            pl.BlockSpec(memory_space=pltpu.SMEM),
            pl.BlockSpec((None, rows, Q_A), cur),
            pl.BlockSpec((None, BLK, KV_A), prev),
            pl.BlockSpec((None, BLK, KV_A), prev),
            pl.BlockSpec((None, rows, KV_A), cur),
            pl.BlockSpec((None, rows, KV_A), cur),
            meta_spec, meta_spec,
        ],
        out_specs=pl.BlockSpec((None, rows, Q_A), cur),
        out_shape=jax.ShapeDtypeStruct((nb, t, Q_A), F32),
        compiler_params=_ARB2,
        name="attn_prompt",
    )(sinks, q, k, v, k, v, k_meta, v_meta)


def _attn_sample_kernel(sinks_ref, q_ref, kp_ref, vp_ref, kc_ref, vc_ref, o_ref, ko_ref, vo_ref, *, sb, tq, n_new):
    rows = lax.broadcasted_iota(jnp.int32, (2 * tq, BLK), 0)
    qi = jnp.where(rows < tq, rows, rows - tq)
    kj = lax.broadcasted_iota(jnp.int32, (2 * tq, BLK), 1)
    mask_p = kj >= qi

    def pad_rows(x):
        return jnp.concatenate([x, jnp.zeros((BLK - tq, LANES), F32)], axis=0)

    items = [(lambda j, s=s: q_ref[s, :, j * LANES:(j + 1) * LANES], kp_ref[s], vp_ref[s],
              pad_rows(kc_ref[s]), pad_rows(vc_ref[s]), mask_p) for s in range(sb)]
    outs = _attend(sinks_ref, items, tq)
    for s in range(sb):
        for j in range(Q_A // LANES):
            o_ref[s, :, j * LANES:(j + 1) * LANES] = outs[s][j]
        ko_ref[s, 0:BLK - n_new, :] = kp_ref[s, n_new:BLK, :]
        ko_ref[s, BLK - n_new:BLK, :] = kc_ref[s, 0:n_new, :]
        vo_ref[s, 0:BLK - n_new, :] = vp_ref[s, n_new:BLK, :]
        vo_ref[s, BLK - n_new:BLK, :] = vc_ref[s, 0:n_new, :]


def _attn_sample(sinks, q, k_new, v_new, k_past, v_past, n_new, sb):
    ns, tq, _ = q.shape
    assert ns % sb == 0

    def idx(g):
        return (g, 0, 0)

    cache = jax.ShapeDtypeStruct((ns, BLK, KV_A), F32)
    return pl.pallas_call(
        functools.partial(_attn_sample_kernel, sb=sb, tq=tq, n_new=n_new),
        grid=(ns // sb,),
        in_specs=[
            pl.BlockSpec(memory_space=pltpu.SMEM),
            pl.BlockSpec((sb, tq, Q_A), idx),
            pl.BlockSpec((sb, BLK, KV_A), idx),
            pl.BlockSpec((sb, BLK, KV_A), idx),
            pl.BlockSpec((sb, tq, KV_A), idx),
            pl.BlockSpec((sb, tq, KV_A), idx),
        ],
        out_specs=[pl.BlockSpec((sb, tq, Q_A), idx), pl.BlockSpec((sb, BLK, KV_A), idx),
                   pl.BlockSpec((sb, BLK, KV_A), idx)],
        out_shape=[jax.ShapeDtypeStruct((ns, tq, Q_A), F32), cache, cache],
        compiler_params=_ARB1,
        name="attn_sample",
    )(sinks, q, k_past, v_past, k_new, v_new)


def _unit_lower_inverse(n_mat, r, c, levels):
    x = (r == c).astype(F32)[None] - jnp.where((((r ^ c) == 1) & (r > c))[None], n_mat, 0.0)
    for lg in range(1, levels):
        same_2g = (r >> (lg + 1)) == (c >> (lg + 1))
        joins = same_2g & (((r >> lg) & 1) == 1) & (((c >> lg) & 1) == 0)
        e = jnp.where(joins[None], n_mat, 0.0).astype(BF16)
        xb = x.astype(BF16)
        x = x - _bmm(_bmm(xb, e).astype(BF16), xb)
    return x


def _decay_gates(ab, alog_ref, dtb_ref, lane):
    xg = ab + dtb_ref[...]
    softplus = jnp.maximum(xg, 0.0) + jnp.log1p(jnp.exp(-jnp.abs(xg)))
    return jnp.where(lane < N_HEADS_B, -jnp.exp(alog_ref[...]) * softplus, _sigmoid(ab))


def _segment_sums(x, sum_mat):
    hi = x.astype(BF16)
    r1 = x - hi.astype(F32)
    mid = r1.astype(BF16)
    lo = (r1 - mid.astype(F32)).astype(BF16)
    s = _dot(sum_mat, lo) + _dot(sum_mat, mid) + _dot(sum_mat, hi)
    return s[:BLK], s[BLK:]


def _delta_prep_kernel(*refs, g_blocks, seg, valid, do_conv):
    rows = g_blocks * BLK
    if do_conv:
        (conv_ref, tailp_ref, tailf_ref, convw_ref, ab_ref, alog_ref, dtb_ref,
         wq_ref, u0_ref, qk_ref, kdt_ref, cdec_ref, n_sc, rhs_sc, cb_sc) = refs
        tail = jnp.where(pl.program_id(1) == 0, tailf_ref[...], tailp_ref[...])
        _store_qkv_b(_conv_silu(tail, conv_ref[...], convw_ref), cb_sc)
        cb = cb_sc
    else:
        cb, ab_ref, alog_ref, dtb_ref, wq_ref, u0_ref, qk_ref, kdt_ref, cdec_ref, n_sc, rhs_sc = refs
    lane = lax.broadcasted_iota(jnp.int32, (rows, LANES), 1)
    gb = _decay_gates(ab_ref[...], alog_ref, dtb_ref, lane)
    _chunk_factors(lambda j, cols: cb[j * BLK:(j + 1) * BLK, cols], lambda j: gb[j * BLK:(j + 1) * BLK],
                   wq_ref, u0_ref, qk_ref, kdt_ref, cdec_ref, n_sc, rhs_sc, g_blocks=g_blocks, seg=seg, valid=valid)


def _chunk_factors(cb_of, gb_of, wq_ref, u0_ref, qk_ref, kdt_ref, cdec_ref, n_sc, rhs_sc, *, g_blocks, seg, valid):
    r = lax.broadcasted_iota(jnp.int32, (BLK, BLK), 0)
    c = lax.broadcasted_iota(jnp.int32, (BLK, BLK), 1)
    incl = r >= c
    strict = r > c
    row_ok = None
    if seg < BLK:
        same_seg = (r & -seg) == (c & -seg)
        incl = incl & same_seg
        strict = strict & same_seg
    if valid is not None:
        pos_r = r & (seg - 1)
        row_ok = (pos_r >= valid[0]) & (pos_r < valid[1])
    chunk = (r & -seg) == (c & -seg)
    sum_mat = jnp.concatenate([jnp.where(incl, 1.0, 0.0), jnp.where(chunk, 1.0, 0.0)],
                              axis=0).astype(BF16)
    for j in range(g_blocks):
        gbj = gb_of(j)
        if row_ok is not None:
            gbj = jnp.where(row_ok, gbj, 0.0)
        gc_cols, tot_cols = _segment_sums(gbj, sum_mat)
        gc_rows = gc_cols.T[0:SUBLANES]
        cdec_ref[j] = jnp.exp(tot_cols.T[0:SUBLANES])
        for h in range(N_HEADS_B):
            qh = cb_of(j, slice(h * DK, (h + 1) * DK))
            kh = cb_of(j, slice(QK_B + h * DK, QK_B + (h + 1) * DK))
            vh = cb_of(j, slice(2 * QK_B + h * DK, 2 * QK_B + (h + 1) * DK))
            if row_ok is not None:
                qh = jnp.where(row_ok, qh, 0.0)
                kh = jnp.where(row_ok, kh, 0.0)
            beta = jnp.broadcast_to(gbj[:, N_HEADS_B + h:N_HEADS_B + h + 1], (BLK, DK))
            gcc = jnp.broadcast_to(gc_cols[:, h:h + 1], (BLK, DK))
            gct = jnp.broadcast_to(tot_cols[:, h:h + 1], (BLK, DK))
            gcr = gc_rows[h:h + 1, :]
            decay = jnp.where(incl, jnp.exp(jnp.where(incl, gcc - gcr, 0.0)), 0.0)
            eg = jnp.exp(gcc)
            kb = kh.astype(BF16)
            n_sc[j * N_HEADS_B + h] = jnp.where(strict, beta * decay * _dot_nt(kb, kb), 0.0)
            rhs_sc[j * N_HEADS_B + h] = jnp.concatenate([beta * eg * kh, beta * vh], axis=1).astype(BF16)
            qk_ref[j, h] = (_dot_nt(qh.astype(BF16), kb) * decay).astype(BF16)
            wq_ref[j, h, BLK:2 * BLK, :] = (qh * eg).astype(BF16)
            kdt_ref[j, h] = (kh * jnp.exp(gct - gcc)).T.astype(BF16)

    t_inv = _unit_lower_inverse(n_sc[...], r, c, levels=seg.bit_length() - 1)
    wu = _bmm(t_inv.astype(BF16), rhs_sc[...])
    for j in range(g_blocks):
        for h in range(N_HEADS_B):
            wq_ref[j, h, 0:BLK, :] = wu[j * N_HEADS_B + h, :, :DK].astype(BF16)
            u0_ref[j, h] = wu[j * N_HEADS_B + h, :, DK:]


def _delta_prep(qkv_b, ab, alog, dtb, g_blocks, seg=BLK, valid=None, conv=None):
    nb, t, _ = qkv_b.shape
    n_blk = t // BLK
    assert n_blk % g_blocks == 0
    rows = g_blocks * BLK
    chains = g_blocks * N_HEADS_B

    def const(shape):
        return pl.BlockSpec(shape, lambda b, j: (0,) * len(shape))

    def out(*tail_shape):
        return pl.BlockSpec((None, g_blocks) + tail_shape, lambda b, j: (b, j) + (0,) * len(tail_shape))

    in_specs = [pl.BlockSpec((None, rows, CONV_DIM), lambda b, j: (b, j, 0))]
    args = [qkv_b]
    scratch = [pltpu.VMEM((chains, BLK, BLK), F32), pltpu.VMEM((chains, BLK, 2 * DK), BF16)]
    if conv is not None:
        in_specs += [pl.BlockSpec((None, SUBLANES, CONV_DIM),
                                  lambda b, j: (b, jnp.maximum(j * (rows // SUBLANES) - 1, 0), 0)),
                     const((SUBLANES, CONV_DIM)), const((CONV_W, CONV_DIM))]
        args += [qkv_b, conv[0], conv[1]]
        scratch += [pltpu.VMEM((rows, CONV_DIM), F32)]
    in_specs += [pl.BlockSpec((None, rows, LANES), lambda b, j: (b, j, 0)), const((1, LANES)), const((1, LANES))]
    args += [ab, alog, dtb]
    kern = functools.partial(_delta_prep_kernel, g_blocks=g_blocks, seg=seg, valid=valid, do_conv=conv is not None)
    return pl.pallas_call(
        kern,
        grid=(nb, n_blk // g_blocks),
        in_specs=in_specs,
        out_specs=[out(N_HEADS_B, 2 * BLK, DK), out(N_HEADS_B, BLK, DK), out(N_HEADS_B, BLK, BLK),
                   out(N_HEADS_B, DK, BLK), out(SUBLANES, BLK)],
        out_shape=[
            jax.ShapeDtypeStruct((nb, n_blk, N_HEADS_B, 2 * BLK, DK), BF16),
            jax.ShapeDtypeStruct((nb, n_blk, N_HEADS_B, BLK, DK), F32),
            jax.ShapeDtypeStruct((nb, n_blk, N_HEADS_B, BLK, BLK), BF16),
            jax.ShapeDtypeStruct((nb, n_blk, N_HEADS_B, DK, BLK), BF16),
            jax.ShapeDtypeStruct((nb, n_blk, SUBLANES, BLK), F32),
        ],
        scratch_shapes=scratch,
        compiler_params=_ARB2,
        name="delta_prep",
    )(*args)


def _gated_norm(o, z, gdnw):
    return _rms(o, gdnw) * (z * _sigmoid(z))


def _delta_fused_kernel(cb_ref, ab_ref, z_ref, s0_ref, alog_ref, dtb_ref, gdnw_ref, o_ref, sout_ref,
                        s_sc, wq_sc, u0_sc, qk_sc, kdt_sc, cdec_sc, n_sc, rhs_sc, *, nbs, group):
    lane = lax.broadcasted_iota(jnp.int32, ab_ref.shape, 2)
    gb = _decay_gates(ab_ref[...], alog_ref, dtb_ref, lane)
    _chunk_factors(lambda j, cols: cb_ref[j, :, cols], lambda j: gb[j],
                   wq_sc, u0_sc, qk_sc, kdt_sc, cdec_sc, n_sc, rhs_sc, g_blocks=nbs, seg=BLK, valid=None)
    _delta_seq_kernel(wq_sc, u0_sc, qk_sc, kdt_sc, cdec_sc, z_ref, s0_ref, gdnw_ref, o_ref, sout_ref, s_sc,
                      nbs=nbs, group=group)


def _delta_fused(qkv_b, ab, z, s0, alog, dtb, gdnw, group):
    nbs, t, _ = qkv_b.shape
    n_blk = t // BLK
    chains = nbs * N_HEADS_B

    def rows(width):
        return pl.BlockSpec((nbs, BLK, width), lambda i: (0, i, 0))

    def const(shape):
        return pl.BlockSpec(shape, lambda i: (0,) * len(shape))

    state = const((nbs, N_HEADS_B, DK, DK))
    return pl.pallas_call(
        functools.partial(_delta_fused_kernel, nbs=nbs, group=group),
        grid=(n_blk,),
        in_specs=[rows(CONV_DIM), rows(LANES), rows(V_B), state, const((1, LANES)), const((1, LANES)),
                  const((1, DK))],
        out_specs=[rows(V_B), state],
        out_shape=[jax.ShapeDtypeStruct((nbs, t, V_B), F32), jax.ShapeDtypeStruct((nbs, N_HEADS_B, DK, DK), F32)],
        scratch_shapes=[
            pltpu.VMEM((nbs, N_HEADS_B, DK, DK), F32),
            pltpu.VMEM((nbs, N_HEADS_B, 2 * BLK, DK), BF16), pltpu.VMEM((nbs, N_HEADS_B, BLK, DK), F32),
            pltpu.VMEM((nbs, N_HEADS_B, BLK, BLK), BF16), pltpu.VMEM((nbs, N_HEADS_B, DK, BLK), BF16),
            pltpu.VMEM((nbs, SUBLANES, BLK), F32),
            pltpu.VMEM((chains, BLK, BLK), F32), pltpu.VMEM((chains, BLK, 2 * DK), BF16),
        ],
        compiler_params=_ARB1,
        name="delta_fused",
    )(qkv_b, ab, z, s0, alog, dtb, gdnw)


def _delta_seq_kernel(wq_ref, u0_ref, qk_ref, kdt_ref, cdec_ref, z_ref, s0_ref, gdnw_ref,
                      o_ref, sout_ref, s_sc, *, nbs, group):
    i = pl.program_id(0)

    @pl.when(i == 0)
    def _():
        s_sc[...] = s0_ref[...]

    gdnw = gdnw_ref[...]
    for b0 in range(0, nbs, group):
        chains = [(b, h) for b in range(b0, min(b0 + group, nbs)) for h in range(N_HEADS_B)]
        s_old = [s_sc[b, h] for b, h in chains]
        ws_qs = [_dot(wq_ref[b, h], s.astype(BF16)) for (b, h), s in zip(chains, s_old)]
        u_new = [(u0_ref[b, h] - x[:BLK]).astype(BF16) for (b, h), x in zip(chains, ws_qs)]
        for (b, h), s, x, u in zip(chains, s_old, ws_qs, u_new):
            o = x[BLK:] + _dot(qk_ref[b, h], u)
            s_sc[b, h] = cdec_ref[b, h:h + 1, :] * s + _dot(kdt_ref[b, h], u)
            o_ref[b, :, h * DK:(h + 1) * DK] = _gated_norm(o, z_ref[b, :, h * DK:(h + 1) * DK], gdnw)

    @pl.when(i == pl.num_programs(0) - 1)
    def _():
        sout_ref[...] = s_sc[...]


def _delta_seq_packed_kernel(wq_ref, u0_ref, qk_ref, kdt_ref, cdec_ref, z_ref, s0_ref, gdnw_ref,
                             o_ref, sout_ref, *, seg):
    n_seq = BLK // seg
    tile = 2 * SUBLANES
    r = lax.broadcasted_iota(jnp.int32, (BLK, DK), 0)
    gdnw = gdnw_ref[...]
    for h in range(N_HEADS_B):
        ws_parts, qs_parts = [], []
        for s in range(n_seq):
            t0 = (s * seg) // tile * tile
            off = s * seg - t0
            lhs = jnp.concatenate([wq_ref[h, t0:t0 + tile, :], wq_ref[h, BLK + t0:BLK + t0 + tile, :]], axis=0)
            x = _dot(lhs, s0_ref[s, h].astype(BF16))
            ws_parts.append(x[off:off + seg])
            qs_parts.append(x[tile + off:tile + off + seg])
        u_new = u0_ref[h] - jnp.concatenate(ws_parts, axis=0)
        o = jnp.concatenate(qs_parts, axis=0) + _dot(qk_ref[h], u_new.astype(BF16))
        o_ref[:, h * DK:(h + 1) * DK] = _gated_norm(o, z_ref[:, h * DK:(h + 1) * DK], gdnw)
        kdt = kdt_ref[h]
        for s in range(n_seq):
            own = (r >= s * seg) & (r < (s + 1) * seg)
            u_own = jnp.where(own, u_new, 0.0).astype(BF16)
            sout_ref[s, h] = cdec_ref[h:h + 1, s * seg:s * seg + 1] * s0_ref[s, h] + _dot(kdt, u_own)


def _delta_seq_packed(wq, u0, qk, kdt, cdec, z, s0, gdnw, seg):
    n_blk = wq.shape[1]
    n_seq = BLK // seg
    assert wq.shape[0] == 1 and s0.shape[0] == n_blk * n_seq

    def blk(*tail_shape):
        return pl.BlockSpec((None, None) + tail_shape, lambda g: (0, g) + (0,) * len(tail_shape))

    state = pl.BlockSpec((n_seq, N_HEADS_B, DK, DK), lambda g: (g, 0, 0, 0))
    rows = pl.BlockSpec((None, BLK, V_B), lambda g: (0, g, 0))
    return pl.pallas_call(
        functools.partial(_delta_seq_packed_kernel, seg=seg),
        grid=(n_blk,),
        in_specs=[
            blk(N_HEADS_B, 2 * BLK, DK), blk(N_HEADS_B, BLK, DK), blk(N_HEADS_B, BLK, BLK),
            blk(N_HEADS_B, DK, BLK), blk(SUBLANES, BLK), rows, state,
            pl.BlockSpec((1, DK), lambda g: (0, 0)),
        ],
        out_specs=[rows, state],
        out_shape=[jax.ShapeDtypeStruct((1, n_blk * BLK, V_B), F32),
                   jax.ShapeDtypeStruct(s0.shape, F32)],
        compiler_params=_ARB1,
        name="delta_seq_packed",
    )(wq, u0, qk, kdt, cdec, z, s0, gdnw)


def _ffn_kernel(x_ref, oa_ref, ob_ref, g_ref, woa_ref, wob_ref, wout_ref, nffn_ref, wg_ref, wu_ref, wd_ref,
                nfin_ref, y_ref, acc_ref):
    ga = _sigmoid(g_ref[:, :D_MODEL])
    gb = _sigmoid(g_ref[:, D_MODEL:])
    merged = ga * _dot(oa_ref[...].astype(BF16), woa_ref[...]) + gb * _dot(ob_ref[...].astype(BF16), wob_ref[...])
    h = x_ref[...] + _dot(merged.astype(BF16), wout_ref[...])
    hn = _rms(h, nffn_ref[...]).astype(BF16)
    acc_ref[...] = h
    for c0 in range(0, D_FF, FF_CHUNK):
        gate = _dot(hn, wg_ref[:, c0:c0 + FF_CHUNK])
        up = _dot(hn, wu_ref[:, c0:c0 + FF_CHUNK])
        act = (gate * _sigmoid(gate) * up).astype(BF16)
        acc_ref[...] += _dot(act, wd_ref[c0:c0 + FF_CHUNK, :])
    y_ref[...] = _rms(acc_ref[...], nfin_ref[...])


def _ffn(x, oa, ob, gates, w_oa, w_ob, w_out, norm_ffn, w_gate, w_up, w_down, norm_final, tm):
    nb, t, _ = x.shape
    assert t % tm == 0

    def row(width):
        return pl.BlockSpec((None, tm, width), lambda b, i: (b, i, 0))

    def const(shape):
        return pl.BlockSpec(shape, lambda b, i: (0, 0), pipeline_mode=pl.Buffered(1))

    return pl.pallas_call(
        _ffn_kernel,
        grid=(nb, t // tm),
        in_specs=[
            row(D_MODEL), row(Q_A), row(V_B), row(2 * D_MODEL),
            const((Q_A, D_MODEL)), const((V_B, D_MODEL)), const((D_MODEL, D_MODEL)), const((1, D_MODEL)),
            const((D_MODEL, D_FF)), const((D_MODEL, D_FF)), const((D_FF, D_MODEL)), const((1, D_MODEL)),
        ],
        out_specs=row(D_MODEL),
        out_shape=jax.ShapeDtypeStruct((nb, t, D_MODEL), F32),
        scratch_shapes=[pltpu.VMEM((tm, D_MODEL), F32)],
        compiler_params=_ARB2,
        name="ffn",
    )(x, oa, ob, gates, w_oa, w_ob, w_out, norm_ffn, w_gate, w_up, w_down, norm_final)


def _rope_tables(pos):
    half = HEAD_DIM // 2
    inv = ROPE_THETA ** (-jnp.arange(half, dtype=F32) / half)
    ang = pos.astype(F32)[:, None] * inv[None, :]
    cos, sin = jnp.cos(ang), jnp.sin(ang)
    reps = LANES // HEAD_DIM
    return (jnp.tile(jnp.concatenate([cos, cos], axis=1), (1, reps)),
            jnp.tile(jnp.concatenate([-sin, sin], axis=1), (1, reps)))


def _pad_seq_rows(x, rows, front=0):
    return jnp.pad(x, ((0, 0), (front, rows - front - x.shape[1]), (0, 0)))


def kernel(x_prompt, x_sample, cache_swa_k, cache_swa_v, state_delta, state_conv, meta_tokens, norm_mix, w_in,
           conv_w, a_log, dt_bias, sinks, gdn_norm, w_oa, w_ob, w_out, norm_ffn, w_gate, w_up, w_down,
           norm_final):
    assert w_in.shape[0] == 1, "single trunk layer"
    nb, seq, _ = x_prompt.shape
    ns, n_new, _ = x_sample.shape
    assert seq % BLK == 0 and n_new <= SUBLANES

    w = w_in[0]
    c_ab = W_MAIN_COLS
    w_packed = (w[:, :c_ab].astype(BF16),
                jnp.pad(w[:, c_ab:c_ab + 2 * N_HEADS_B], ((0, 0), (0, LANES - 2 * N_HEADS_B))).astype(BF16),
                w[:, c_ab + 2 * N_HEADS_B:].astype(BF16))
    nmix = norm_mix[0][None, :]
    lane_pad = (0, LANES - N_HEADS_B)
    alog = jnp.pad(a_log[0], lane_pad)[None, :]
    dtb = jnp.pad(dt_bias[0], lane_pad)[None, :]
    gdnw = gdn_norm[0][None, :]
    cw = conv_w[0]
    ffn_w = (w_oa[0].astype(BF16), w_ob[0].astype(BF16), w_out[0].astype(BF16), norm_ffn[0][None, :],
             w_gate[0].astype(BF16), w_up[0].astype(BF16), w_down[0].astype(BF16), norm_final[None, :])
    sk = sinks[0]

    x_meta = jnp.concatenate([jnp.zeros((PAD_ROWS, D_MODEL), F32), meta_tokens.astype(F32)], axis=0)[None]
    cos_m, sin_m = _rope_tables(jnp.arange(BLK, dtype=jnp.int32) - PAD_ROWS)
    zero_tail = jnp.zeros((SUBLANES, CONV_DIM), F32)
    _, k_m, v_m, qkvb_m, z_m, ab_m, _, tail_m = _proj(x_meta, nmix, w_packed, cos_m, sin_m, tm=BLK,
                                                     conv=(zero_tail, cw))
    _, s_meta = _delta_fused(qkvb_m, ab_m, z_m, jnp.zeros((1, N_HEADS_B, DK, DK), F32), alog, dtb, gdnw, group=1)

    cos_p, sin_p = _rope_tables(N_META + jnp.arange(seq, dtype=jnp.int32))
    q_p, k_p, v_p, qkvb_p, z_p, ab_p, gate_p, tail_p = _proj(x_prompt, nmix, w_packed, cos_p, sin_p, tm=4 * BLK,
                                                            conv=(tail_m[0, 0], cw))
    oa_p = _attn_prompt(sk, q_p, k_p, v_p, k_m[0], v_m[0], g_blocks=8)
    ob_p, s_p = _delta_fused(qkvb_p, ab_p, z_p, jnp.broadcast_to(s_meta, (nb, N_HEADS_B, DK, DK)), alog, dtb,
                             gdnw, group=4)
    y_prompt = _ffn(x_prompt, oa_p, ob_p, gate_p, *ffn_w, tm=4 * BLK)

    rows_s = ns * n_new
    pos_s = PAST_LEN + (jnp.arange(rows_s, dtype=jnp.int32) % n_new)
    cos_s, sin_s = _rope_tables(pos_s)
    outs = _proj(x_sample.reshape(1, rows_s, D_MODEL), nmix, w_packed, cos_s, sin_s, tm=rows_s // 2)
    q_s, k_s, v_s, conv_s, z_s, ab_s, gate_s = [o.reshape(ns, n_new, o.shape[-1]) for o in outs]
    pad8 = functools.partial(_pad_seq_rows, rows=SUBLANES)
    oa_s, k_cache, v_cache = _attn_sample(
        sk, pad8(q_s), pad8(k_s), pad8(v_s), cache_swa_k[0].reshape(ns, BLK, KV_A),
        cache_swa_v[0].reshape(ns, BLK, KV_A), n_new, sb=8)
    n_hist = CONV_W - 1
    assert n_hist + n_new <= SUBLANES and (ns * SUBLANES) % BLK == 0
    front = functools.partial(_pad_seq_rows, rows=SUBLANES, front=n_hist)
    conv_pk = pad8(jnp.concatenate([state_conv[0], conv_s], axis=1)).reshape(1, ns * SUBLANES, CONV_DIM)
    ab_pk = front(ab_s).reshape(1, ns * SUBLANES, LANES)
    z_pk = front(z_s).reshape(1, ns * SUBLANES, V_B)
    n_blk_s = ns * SUBLANES // BLK
    prep_s = _delta_prep(conv_pk, ab_pk, alog, dtb, g_blocks=4 if n_blk_s % 4 == 0 else 1,
                         seg=SUBLANES, valid=(n_hist, n_hist + n_new), conv=(zero_tail, cw))
    ob_pk, s_s = _delta_seq_packed(*prep_s, z_pk, state_delta[0], gdnw, seg=SUBLANES)
    ob_s = ob_pk.reshape(ns, SUBLANES, V_B)[:, n_hist:n_hist + n_new]
    y_sample = _ffn(x_sample.reshape(1, rows_s, D_MODEL), oa_s[:, :n_new].reshape(1, rows_s, Q_A),
                    ob_s.reshape(1, rows_s, V_B), gate_s.reshape(1, rows_s, 2 * D_MODEL),
                    *ffn_w, tm=rows_s // 2).reshape(ns, n_new, D_MODEL)

    kv_shape = (1, -1, BLK, N_KV_HEADS, HEAD_DIM)
    conv_new_s = jnp.concatenate([state_conv[0], conv_s], axis=1)[:, -(CONV_W - 1):]
    return (y_prompt, y_sample,
            k_p[:, seq - BLK:].reshape(kv_shape), v_p[:, seq - BLK:].reshape(kv_shape),
            s_p[None], tail_p[:, -1, SUBLANES - (CONV_W - 1):][None],
            k_cache.reshape(kv_shape), v_cache.reshape(kv_shape), s_s[None], conv_new_s[None])
```
